```python
import math
import jax, jax.numpy as jnp
from jax import lax
import numpy as np

D_MODEL = 2048
BATCH = 4
SEQ = 2048
DEPTH = 4

N_MIXERS = 3
N_HEADS = 16
HEAD_DIM = D_MODEL // N_HEADS
D_FF = 5632
PLE_DIM = 256
Q_BLOCK = 128
MOBA_BLOCK = 256
MOBA_TOP_K = 3
MOBA_Q_CHUNK = 16
S5_GROUP = 16
S5_GROUPS = D_MODEL // S5_GROUP
S5_STATE = 64
RMS_EPS = 1e-6
N_FOX = len(range(0, DEPTH, N_MIXERS))
N_MOBA = len(range(1, DEPTH, N_MIXERS))
N_S5 = len(range(2, DEPTH, N_MIXERS))

kernel_name = "hybrid_fox_moba_s5_macaron_ple"


def _rmsnorm(x, g):
    xf = x.astype(jnp.float32)
    y = xf * lax.rsqrt(jnp.mean(xf * xf, axis=-1, keepdims=True) + RMS_EPS)
    return (y * g.astype(jnp.float32)).astype(x.dtype)


def _swiglu(x, w_in, w_out):
    gate, up = jnp.split(x @ w_in, 2, axis=-1)
    return (jax.nn.silu(gate) * up) @ w_out


def _fox_attention(xn, w_in, b_f, w_o):
    B, S, D = xn.shape
    H, Dh = N_HEADS, HEAD_DIM
    proj = xn @ w_in
    q, k, v, f_logit = jnp.split(proj, [D, 2 * D, 3 * D], axis=-1)
    q = q.reshape(B, S, H, Dh)
    k = k.reshape(B, S, H, Dh)
    v = v.reshape(B, S, H, Dh)
    log_f = jax.nn.log_sigmoid((f_logit + b_f).astype(jnp.float32))
    c = jnp.cumsum(log_f, axis=1).transpose(0, 2, 1)
    scale = Dh ** -0.5
    kpos = jnp.arange(S)

    def q_block(i):
        t0 = i * Q_BLOCK
        qb = lax.dynamic_slice_in_dim(q, t0, Q_BLOCK, axis=1)
        cb = lax.dynamic_slice_in_dim(c, t0, Q_BLOCK, axis=2)
        logits = jnp.einsum('bqhd,bkhd->bhqk', qb, k).astype(jnp.float32) * scale
        logits = logits + cb[..., None] - c[:, :, None, :]
        qpos = t0 + jnp.arange(Q_BLOCK)
        logits = jnp.where(kpos[None, :] <= qpos[:, None], logits, -jnp.inf)
        probs = jax.nn.softmax(logits, axis=-1).astype(v.dtype)
        return jnp.einsum('bhqk,bkhd->bqhd', probs, v)

    out = lax.map(q_block, jnp.arange(S // Q_BLOCK))
    out = out.transpose(1, 0, 2, 3, 4).reshape(B, S, D)
    return out @ w_o


def _moba_attention(xn, w_qkv, w_o):
    B, S, D = xn.shape
    H, Dh, MB = N_HEADS, HEAD_DIM, MOBA_BLOCK
    n_blk = -(-S // MB)
    s_pad = n_blk * MB
    n_sel = min(MOBA_TOP_K, n_blk)
    q, k, v = jnp.split(xn @ w_qkv, 3, axis=-1)
    q = q.reshape(B, S, H, Dh).transpose(0, 2, 1, 3)
    k = k.reshape(B, S, H, Dh).transpose(0, 2, 1, 3)
    v = v.reshape(B, S, H, Dh).transpose(0, 2, 1, 3)
    pad = ((0, 0), (0, 0), (0, s_pad - S), (0, 0))
    kb = jnp.pad(k, pad).reshape(B, H, n_blk, MB, Dh)
    vb = jnp.pad(v, pad).reshape(B, H, n_blk, MB, Dh)
    k_mean = jnp.mean(kb.astype(jnp.float32), axis=3)
    slopes = 2.0 ** (-8.0 * jnp.arange(1, H + 1, dtype=jnp.float32) / H)
    scale = Dh ** -0.5
    b_idx = jnp.arange(B)[:, None, None, None]
    h_idx = jnp.arange(H)[None, :, None, None]
    blk_ids = jnp.arange(n_blk)
    in_blk = jnp.arange(MB)

    def q_chunk(i):
        t0 = i * MOBA_Q_CHUNK
        qc = lax.dynamic_slice_in_dim(q, t0, MOBA_Q_CHUNK, axis=2)
        qpos = t0 + jnp.arange(MOBA_Q_CHUNK)
        own = t0 // MB
        gate = jnp.einsum('bhqd,bhnd->bhqn', qc.astype(jnp.float32), k_mean)
        gate = jnp.where(blk_ids < own, gate, -jnp.inf)
        _, sel = lax.top_k(gate, n_sel)
        sel_ok = sel < own
        k_sel = kb[b_idx, h_idx, sel]
        v_sel = vb[b_idx, h_idx, sel]
        lg_sel = jnp.einsum('bhqd,bhqnkd->bhqnk', qc, k_sel).astype(jnp.float32) * scale
        pos_sel = sel[..., None] * MB + in_blk
        dist_sel = (qpos[None, None, :, None, None] - pos_sel).astype(jnp.float32)
        lg_sel = lg_sel - slopes[None, :, None, None, None] * dist_sel
        lg_sel = jnp.where(sel_ok[..., None], lg_sel, -jnp.inf)
        lg_sel = lg_sel.reshape(B, H, MOBA_Q_CHUNK, n_sel * MB)
        k_own = lax.dynamic_slice_in_dim(kb, own, 1, axis=2)[:, :, 0]
        v_own = lax.dynamic_slice_in_dim(vb, own, 1, axis=2)[:, :, 0]
        lg_own = jnp.einsum('bhqd,bhkd->bhqk', qc, k_own).astype(jnp.float32) * scale
        dist_own = qpos[:, None] - (own * MB + in_blk)[None, :]
        lg_own = lg_own - slopes[None, :, None, None] * dist_own.astype(jnp.float32)
        lg_own = jnp.where(dist_own >= 0, lg_own, -jnp.inf)
        probs = jax.nn.softmax(jnp.concatenate([lg_sel, lg_own], axis=-1), axis=-1).astype(v.dtype)
        p_sel = probs[..., :n_sel * MB].reshape(B, H, MOBA_Q_CHUNK, n_sel, MB)
        p_own = probs[..., n_sel * MB:]
        return (jnp.einsum('bhqnk,bhqnkd->bhqd', p_sel, v_sel)
                + jnp.einsum('bhqk,bhkd->bhqd', p_own, v_own))

    out = lax.map(q_chunk, jnp.arange(S // MOBA_Q_CHUNK))
    out = out.transpose(1, 0, 3, 2, 4).reshape(B, S, D)
    return out @ w_o


def _s5_layer(xn, w_in, a_re, a_im, log_dt, b_re, b_im, c_re, c_im, d_skip, w_glu):
    B, S, D = xn.shape
    G, P, Cg = S5_GROUPS, S5_STATE, S5_GROUP
    f32 = jnp.float32
    u = (xn @ w_in).astype(f32).reshape(B, S, G, Cg)
    A = lax.complex(a_re.astype(f32), a_im.astype(f32))
    dt = jnp.exp(log_dt.astype(f32))[:, None]
    A_bar = jnp.exp(A * dt)
    B_mat = lax.complex(b_re.astype(f32), b_im.astype(f32))
    B_bar = ((A_bar - 1.0) / A)[..., None] * B_mat
    Bu = jnp.einsum('gpc,bsgc->sbgp', B_bar, u.astype(jnp.complex64))
    a_elems = jnp.broadcast_to(A_bar, (S, 1, G, P))

    def combine(left, right):
        a_l, b_l = left
        a_r, b_r = right
        return a_l * a_r, a_r * b_l + b_r

    _, h = lax.associative_scan(combine, (a_elems, Bu), axis=0)
    C_mat = lax.complex(c_re.astype(f32), c_im.astype(f32))
    y = jnp.einsum('gcp,sbgp->bsgc', C_mat, h).real
    y = y + d_skip.astype(f32).reshape(G, Cg) * u
    y = jax.nn.gelu(y.reshape(B, S, D)).astype(xn.dtype)
    val, gate = jnp.split(y @ w_glu, 2, axis=-1)
    return val * jax.nn.sigmoid(gate)


def setup_inputs(seed: int = 0) -> dict:
    key = jax.random.key(seed)
    ks = jax.random.split(key, 24)
    D, H, F = D_MODEL, N_HEADS, D_FF
    G, P, Cg = S5_GROUPS, S5_STATE, S5_GROUP
    nrm = jax.random.normal
    x = nrm(ks[0], (BATCH, SEQ, D), jnp.float32)
    p = nrm(ks[1], (DEPTH, BATCH, SEQ, PLE_DIM), jnp.float32)
    norm_g = 1.0 + 0.05 * nrm(ks[2], (DEPTH, 4, D), jnp.float32)
    final_g = 1.0 + 0.05 * nrm(ks[3], (D,), jnp.float32)
    w_ffn_in = nrm(ks[4], (DEPTH, 2, D, 2 * F), jnp.float32) * D ** -0.5
    w_ffn_out = nrm(ks[5], (DEPTH, 2, F, D), jnp.float32) * F ** -0.5
    w_ple_gate = nrm(ks[6], (DEPTH, D, D), jnp.float32) * D ** -0.5
    w_ple_proj = nrm(ks[7], (DEPTH, PLE_DIM, D), jnp.float32) * PLE_DIM ** -0.5
    fox_w_in = nrm(ks[8], (N_FOX, D, 3 * D + H), jnp.float32) * D ** -0.5
    fox_b_f = jax.random.uniform(ks[9], (N_FOX, H), jnp.float32, 1.0, 6.0)
    fox_w_o = nrm(ks[10], (N_FOX, D, D), jnp.float32) * D ** -0.5
    moba_w_qkv = nrm(ks[11], (N_MOBA, D, 3 * D), jnp.float32) * D ** -0.5
    moba_w_o = nrm(ks[12], (N_MOBA, D, D), jnp.float32) * D ** -0.5
    s5_w_in = nrm(ks[13], (N_S5, D, D), jnp.float32) * D ** -0.5
    s5_a_re = -0.5 + 0.01 * nrm(ks[14], (N_S5, G, P), jnp.float32)
    s5_a_im = jnp.broadcast_to(math.pi * jnp.arange(P, dtype=jnp.float32), (N_S5, G, P))
    s5_log_dt = jax.random.uniform(ks[15], (N_S5, G), jnp.float32, math.log(1e-3), math.log(1e-1))
    s5_b_re = nrm(ks[16], (N_S5, G, P, Cg), jnp.float32) * (2.0 * Cg) ** -0.5
    s5_b_im = nrm(ks[17], (N_S5, G, P, Cg), jnp.float32) * (2.0 * Cg) ** -0.5
    s5_c_re = nrm(ks[18], (N_S5, G, Cg, P), jnp.float32) * (2.0 * P) ** -0.5 * 4.0
    s5_c_im = nrm(ks[19], (N_S5, G, Cg, P), jnp.float32) * (2.0 * P) ** -0.5 * 4.0
    s5_d = nrm(ks[20], (N_S5, D), jnp.float32)
    s5_w_glu = nrm(ks[21], (N_S5, D, 2 * D), jnp.float32) * D ** -0.5
    return {"x": x, "p": p, "norm_g": norm_g, "final_g": final_g,
            "w_ffn_in": w_ffn_in, "w_ffn_out": w_ffn_out,
            "w_ple_gate": w_ple_gate, "w_ple_proj": w_ple_proj,
            "fox_w_in": fox_w_in, "fox_b_f": fox_b_f, "fox_w_o": fox_w_o,
            "moba_w_qkv": moba_w_qkv, "moba_w_o": moba_w_o,
            "s5_w_in": s5_w_in, "s5_a_re": s5_a_re, "s5_a_im": s5_a_im,
            "s5_log_dt": s5_log_dt, "s5_b_re": s5_b_re, "s5_b_im": s5_b_im,
            "s5_c_re": s5_c_re, "s5_c_im": s5_c_im, "s5_d": s5_d, "s5_w_glu": s5_w_glu}


def reference(x, p, norm_g, final_g, w_ffn_in, w_ffn_out, w_ple_gate, w_ple_proj,
              fox_w_in, fox_b_f, fox_w_o, moba_w_qkv, moba_w_o,
              s5_w_in, s5_a_re, s5_a_im, s5_log_dt, s5_b_re, s5_b_im,
              s5_c_re, s5_c_im, s5_d, s5_w_glu):
    h = x
    for i in range(DEPTH):
        g = norm_g[i]
        h = h + 0.5 * _swiglu(_rmsnorm(h, g[0]), w_ffn_in[i, 0], w_ffn_out[i, 0])
        xn = _rmsnorm(h, g[1])
        kind, j = i % N_MIXERS, i // N_MIXERS
        if kind == 0:
            mix = _fox_attention(xn, fox_w_in[j], fox_b_f[j], fox_w_o[j])
        elif kind == 1:
            mix = _moba_attention(xn, moba_w_qkv[j], moba_w_o[j])
        else:
            mix = _s5_layer(xn, s5_w_in[j], s5_a_re[j], s5_a_im[j], s5_log_dt[j],
                            s5_b_re[j], s5_b_im[j], s5_c_re[j], s5_c_im[j], s5_d[j], s5_w_glu[j])
        h = h + mix
        h = h + 0.5 * _swiglu(_rmsnorm(h, g[2]), w_ffn_in[i, 1], w_ffn_out[i, 1])
        ple_gate = jax.nn.sigmoid(_rmsnorm(h, g[3]) @ w_ple_gate[i])
        h = h + ple_gate * (p[i] @ w_ple_proj[i])
    return _rmsnorm(h, final_g)
```

```python
import functools
import math

import jax
import jax.numpy as jnp
from jax import lax
from jax.experimental import pallas as pl
from jax.experimental.pallas import tpu as pltpu

F32 = jnp.float32
BF16 = jnp.bfloat16

N_HEADS = 16
HEAD_DIM = 128
MOBA_BLOCK = 256
MOBA_TOP_K = 3
S5_GROUP = 16
S5_STATE = 64
S5_CHUNK = 16
RMS_EPS = 1e-6
MASK_VALUE = -1e30

V7X_VMEM_BYTES = 64 * 1024 * 1024
VMEM_LIMIT_BYTES = V7X_VMEM_BYTES - 8 * 1024 * 1024
PROLOGUE_ROWS = 256


def _compiler_params(semantics):
    return pltpu.CompilerParams(dimension_semantics=semantics,
                                vmem_limit_bytes=VMEM_LIMIT_BYTES)


def _sigmoid(x):
    return 1.0 / (1.0 + jnp.exp(-x))


def _rms_rows(x, g):
    ms = jnp.mean(x * x, axis=-1, keepdims=True)
    return x * lax.rsqrt(ms + RMS_EPS) * g


def _fused_matmul(a_list, row_consts, prologue, lhs_widths, terms, extras,
                  epilogue, out_dtype, n_out, tm, tn):
    m_rows = a_list[0].shape[0]
    grid = (m_rows // tm, n_out // tn)
    n_a, n_c, n_t, n_e = len(a_list), len(row_consts), len(terms), len(extras)

    def body(*refs):
        a_refs = refs[:n_a]
        c_refs = refs[n_a:n_a + n_c]
        w_refs = refs[n_a + n_c:n_a + n_c + n_t]
        e_refs = refs[n_a + n_c + n_t:n_a + n_c + n_t + n_e]
        o_ref = refs[n_a + n_c + n_t + n_e]
        lhs_refs = refs[n_a + n_c + n_t + n_e + 1:]
        if prologue is not None:
            @pl.when(pl.program_id(1) == 0)
            def _():
                consts = [c[...] for c in c_refs]

                def rows_step(r, carry):
                    rows = pl.ds(pl.multiple_of(r * PROLOGUE_ROWS, PROLOGUE_ROWS),
                                 PROLOGUE_ROWS)
                    outs = prologue([a[rows, :] for a in a_refs], consts)
                    for dst, val in zip(lhs_refs, outs):
                        dst[rows, :] = val
                    return carry

                lax.fori_loop(0, tm // PROLOGUE_ROWS, rows_step, 0)
        else:
            lhs_refs = a_refs
        accs = [jnp.dot(lhs_refs[li][...], w[...], preferred_element_type=F32)
                for (li, _, _), w in zip(terms, w_refs)]
        o_ref[...] = epilogue(accs, [e[...] for e in e_refs]).astype(out_dtype)

    in_specs = [pl.BlockSpec((tm, a.shape[1]), lambda i, j: (i, 0)) for a in a_list]
    in_specs += [pl.BlockSpec((1, c.shape[1]), lambda i, j: (0, 0)) for c in row_consts]
    for _, w, off in terms:
        in_specs.append(pl.BlockSpec((w.shape[0], tn),
                                     functools.partial(lambda i, j, o: (0, j + o), o=off)))
    for arr, kind in extras:
        if kind == 'tile':
            in_specs.append(pl.BlockSpec((tm, tn), lambda i, j: (i, j)))
        else:
            in_specs.append(pl.BlockSpec((1, tn), lambda i, j: (0, j)))
    scratch = []
    if prologue is not None:
        scratch = [pltpu.VMEM((tm, k), BF16) for k in lhs_widths]
    return pl.pallas_call(
        body,
        grid=grid,
        in_specs=in_specs,
        out_specs=pl.BlockSpec((tm, tn), lambda i, j: (i, j)),
        out_shape=jax.ShapeDtypeStruct((m_rows, n_out), out_dtype),
        scratch_shapes=scratch,
        compiler_params=_compiler_params(("parallel", "arbitrary")),
    )(*a_list, *row_consts, *[w for _, w, _ in terms], *[e for e, _ in extras])


def _norm_prologue(a_tiles, consts):
    return [_rms_rows(a_tiles[0], consts[0]).astype(BF16)]


def _ffn(h, gain, w_in, w_out, tm_in, tn_in, tm_out, tn_out):
    d_model = h.shape[1]
    d_ff = w_out.shape[0]

    def act_epilogue(accs, _):
        gate, up = accs
        return gate * _sigmoid(gate) * up

    act = _fused_matmul([h], [gain.reshape(1, d_model)], _norm_prologue, [d_model],
                        [(0, w_in, 0), (0, w_in, d_ff // tn_in)], [],
                        act_epilogue, BF16, d_ff, tm_in, tn_in)

    def res_epilogue(accs, ex):
        return ex[0] + 0.5 * accs[0]

    return _fused_matmul([act], [], None, [], [(0, w_out, 0)], [(h, 'tile')],
                         res_epilogue, F32, d_model, tm_out, tn_out)


def _ple(h, gain, p_i, w_gate, w_proj, tm, tn):
    d_model = h.shape[1]

    def prologue(a_tiles, consts):
        return [_rms_rows(a_tiles[0], consts[0]).astype(BF16), a_tiles[1].astype(BF16)]

    def epilogue(accs, ex):
        return ex[0] + _sigmoid(accs[0]) * accs[1]

    return _fused_matmul([h, p_i], [gain.reshape(1, d_model)], prologue,
                         [d_model, p_i.shape[1]],
                         [(0, w_gate, 0), (1, w_proj, 0)], [(h, 'tile')],
                         epilogue, F32, d_model, tm, tn)


def _norm_proj(h, gain, w, out_dtype, tm, tn, n_out=None):
    d_model = h.shape[1]
    return _fused_matmul([h], [gain.reshape(1, d_model)], _norm_prologue, [d_model],
                         [(0, w, 0)], [], lambda accs, _: accs[0], out_dtype,
                         n_out or w.shape[1], tm, tn)


def _out_proj(a, w, h, tm, tn):
    return _fused_matmul([a], [], None, [], [(0, w, 0)], [(h, 'tile')],
                         lambda accs, ex: ex[0] + accs[0], F32, w.shape[1], tm, tn)


def _final_norm(h, gain, tm):
    m_rows, d_model = h.shape

    def body(h_ref, g_ref, o_ref):
        o_ref[...] = _rms_rows(h_ref[...], g_ref[...])

    return pl.pallas_call(
        body,
        grid=(m_rows // tm,),
        in_specs=[pl.BlockSpec((tm, d_model), lambda i: (i, 0)),
                  pl.BlockSpec((1, d_model), lambda i: (0, 0))],
        out_specs=pl.BlockSpec((tm, d_model), lambda i: (i, 0)),
        out_shape=jax.ShapeDtypeStruct((m_rows, d_model), F32),
        compiler_params=_compiler_params(("parallel",)),
    )(h, gain.reshape(1, d_model))


def _log_forget_cumsum(logf):
    n_b, seq, lanes = logf.shape

    def body(x_ref, o_ref):
        x = x_ref[0]
        row = lax.broadcasted_iota(jnp.int32, (seq, lanes), 0)
        shift = 1
        while shift < seq:
            x = x + jnp.where(row >= shift, pltpu.roll(x, shift, 0), 0.0)
            shift *= 2
        o_ref[0] = x

    return pl.pallas_call(
        body,
        grid=(n_b,),
        in_specs=[pl.BlockSpec((1, seq, lanes), lambda b: (b, 0, 0))],
        out_specs=pl.BlockSpec((1, seq, lanes), lambda b: (b, 0, 0)),
        out_shape=jax.ShapeDtypeStruct((n_b, seq, lanes), F32),
        compiler_params=_compiler_params(("parallel",)),
    )(logf)


def _online_softmax_step(carry, s, v):
    m, l, acc = carry
    m_new = jnp.maximum(m, jnp.max(s, axis=1, keepdims=True))
    alpha = jnp.exp(m - m_new)
    p = jnp.exp(s - m_new)
    l = alpha * l + jnp.sum(p, axis=1, keepdims=True)
    acc = alpha * acc + jnp.dot(p.astype(BF16), v, preferred_element_type=F32)
    return m_new, l, acc


def _first_softmax_step(s, v):
    m = jnp.max(s, axis=1, keepdims=True)
    p = jnp.exp(s - m)
    l = jnp.sum(p, axis=1, keepdims=True)
    acc = jnp.dot(p.astype(BF16), v, preferred_element_type=F32)
    return m, l, acc


def _qk(q, k):
    return lax.dot_general(q, k, (((1,), (1,)), ((), ())), preferred_element_type=F32)


def _fox_attention(qkv, c_col, c_row, tq):
    n_b, seq, _ = qkv.shape
    n_q = seq // tq
    scale = HEAD_DIM ** -0.5

    def body(q_ref, k_ref, v_ref, cq_ref, ck_ref, o_ref):
        qi = pl.program_id(2)
        q = q_ref[0]
        cq = cq_ref[0, 0]

        def logits(j):
            rows = pl.ds(pl.multiple_of(j * tq, tq), tq)
            s = _qk(q, k_ref[0, rows, :]) * scale
            return s + (cq - ck_ref[0, 0, j]), v_ref[0, rows, :]

        s, v = logits(qi)
        row = lax.broadcasted_iota(jnp.int32, (tq, tq), 0)
        col = lax.broadcasted_iota(jnp.int32, (tq, tq), 1)
        s = jnp.where(row >= col, s, MASK_VALUE)
        carry = _first_softmax_step(s, v)

        def past_block(j, carry):
            s, v = logits(j)
            return _online_softmax_step(carry, s, v)

        _, l, acc = lax.fori_loop(0, qi, past_block, carry)
        o_ref[0] = (acc / l).astype(BF16)

    return pl.pallas_call(
        body,
        grid=(n_b, N_HEADS, n_q),
        in_specs=[
            pl.BlockSpec((1, tq, HEAD_DIM), lambda b, h, i: (b, i, h)),
            pl.BlockSpec((1, seq, HEAD_DIM), lambda b, h, i: (b, 0, N_HEADS + h)),
            pl.BlockSpec((1, seq, HEAD_DIM), lambda b, h, i: (b, 0, 2 * N_HEADS + h)),
            pl.BlockSpec((1, 1, tq, 1), lambda b, h, i: (b, h, i, 0)),
            pl.BlockSpec((1, 1, n_q, 1, tq), lambda b, h, i: (b, h, 0, 0, 0)),
        ],
        out_specs=pl.BlockSpec((1, tq, HEAD_DIM), lambda b, h, i: (b, i, h)),
        out_shape=jax.ShapeDtypeStruct((n_b, seq, N_HEADS * HEAD_DIM), BF16),
        compiler_params=_compiler_params(("parallel", "parallel", "arbitrary")),
    )(qkv, qkv, qkv, c_col, c_row)


def _fox_layer(h, gain, w_in, b_f, w_o, n_b, seq, tiles):
    d_model = h.shape[1]
    tm, tn = tiles
    qkv = _norm_proj(h, gain, w_in, BF16, tm, tn, n_out=3 * d_model)
    w_f = jnp.pad(w_in[:, 3 * d_model:], ((0, 0), (0, HEAD_DIM - N_HEADS)))
    b_row = jnp.pad(b_f, (0, HEAD_DIM - N_HEADS)).reshape(1, HEAD_DIM)

    def logsig_epilogue(accs, ex):
        z = accs[0] + ex[0]
        return jnp.minimum(z, 0.0) - jnp.log(1.0 + jnp.exp(-jnp.abs(z)))

    logf = _fused_matmul([h], [gain.reshape(1, d_model)], _norm_prologue, [d_model],
                         [(0, w_f, 0)], [(b_row, 'row')], logsig_epilogue, F32,
                         HEAD_DIM, tm, HEAD_DIM)
    c = _log_forget_cumsum(logf.reshape(n_b, seq, HEAD_DIM))[:, :, :N_HEADS]
    c = c.transpose(0, 2, 1)
    tq = 256
    attn = _fox_attention(qkv.reshape(n_b, seq, 3 * d_model),
                          c.reshape(n_b, N_HEADS, seq, 1),
                          c.reshape(n_b, N_HEADS, seq // tq, 1, tq), tq)
    return _out_proj(attn.reshape(n_b * seq, d_model), w_o, h, tm, tn)


def _moba_attention(qkv, slopes):
    n_b, seq, _ = qkv.shape
    blk = MOBA_BLOCK
    n_blk = seq // blk
    scale = HEAD_DIM ** -0.5

    def body(slopes_ref, q_ref, k_ref, v_ref, o_ref, kmean_ref):
        h = pl.program_id(1)
        qi = pl.program_id(2)

        @pl.when(qi == 0)
        def _():
            kmean_ref[...] = jnp.zeros_like(kmean_ref)
            for n in range(n_blk):
                kb = k_ref[0, n * blk:(n + 1) * blk, :].astype(F32)
                kmean_ref[n:n + 1, :] = jnp.mean(kb, axis=0, keepdims=True)

        q = q_ref[0]
        slope = slopes_ref[h]
        gate = _qk(kmean_ref[...].astype(BF16), q)
        rows = [gate[n:n + 1, :] for n in range(n_blk)]
        past = [(n < qi).astype(F32) for n in range(n_blk)]
        bias_rows = []
        for n in range(n_blk):
            beaten = jnp.zeros((1, blk), F32)
            for n2 in range(n_blk):
                if n2 == n:
                    continue
                wins = (rows[n2] >= rows[n]) if n2 < n else (rows[n2] > rows[n])
                beaten = beaten + jnp.where(wins, past[n2], 0.0)
            chosen = jnp.where(beaten < MOBA_TOP_K, past[n], 0.0)
            bias_rows.append((1.0 - chosen) * MASK_VALUE)
        bias_t = jnp.concatenate(
            bias_rows + [jnp.zeros((HEAD_DIM - n_blk, blk), F32)], axis=0)
        bias = bias_t.T
        lane = lax.broadcasted_iota(jnp.int32, (blk, HEAD_DIM), 1)

        row = lax.broadcasted_iota(jnp.int32, (blk, blk), 0)
        col = lax.broadcasted_iota(jnp.int32, (blk, blk), 1)
        rel = row - col
        alibi = (-slope) * rel.astype(F32)

        def logits(j):
            keys = pl.ds(pl.multiple_of(j * blk, blk), blk)
            return _qk(q, k_ref[0, keys, :]) * scale + alibi, v_ref[0, keys, :]

        s, v = logits(qi)
        s = jnp.where(rel >= 0, s, MASK_VALUE)
        carry = _first_softmax_step(s, v)

        def past_block(j, carry):
            s, v = logits(j)
            block_bias = jnp.sum(jnp.where(lane == j, bias, 0.0), axis=1, keepdims=True)
            s = s + (block_bias - slope * ((qi - j) * blk).astype(F32))
            return _online_softmax_step(carry, s, v)

        _, l, acc = lax.fori_loop(0, qi, past_block, carry)
        o_ref[0] = (acc / l).astype(BF16)

    return pl.pallas_call(
        body,
        grid=(n_b, N_HEADS, n_blk),
        in_specs=[
            pl.BlockSpec(memory_space=pltpu.SMEM),
            pl.BlockSpec((1, blk, HEAD_DIM), lambda b, h, i: (b, i, h)),
            pl.BlockSpec((1, seq, HEAD_DIM), lambda b, h, i: (b, 0, N_HEADS + h)),
            pl.BlockSpec((1, seq, HEAD_DIM), lambda b, h, i: (b, 0, 2 * N_HEADS + h)),
        ],
        out_specs=pl.BlockSpec((1, blk, HEAD_DIM), lambda b, h, i: (b, i, h)),
        out_shape=jax.ShapeDtypeStruct((n_b, seq, N_HEADS * HEAD_DIM), BF16),
        scratch_shapes=[pltpu.VMEM((HEAD_DIM, HEAD_DIM), F32)],
        compiler_params=_compiler_params(("parallel", "parallel", "arbitrary")),
    )(slopes, qkv, qkv, qkv)


def _moba_layer(h, gain, w_qkv, w_o, n_b, seq, tiles):
    d_model = h.shape[1]
    tm, tn = tiles
    qkv = _norm_proj(h, gain, w_qkv, BF16, tm, tn)
    slopes = 2.0 ** (-8.0 * jnp.arange(1, N_HEADS + 1, dtype=F32) / N_HEADS)
    attn = _moba_attention(qkv.reshape(n_b, seq, 3 * d_model), slopes)
    return _out_proj(attn.reshape(n_b * seq, d_model), w_o, h, tm, tn)


def _s5_matrices(a_re, a_im, log_dt, b_re, b_im, c_re, c_im):
    n_g, n_p = a_re.shape
    t_len = S5_CHUNK
    a = lax.complex(a_re.astype(F32), a_im.astype(F32))
    dt = jnp.exp(log_dt.astype(F32))[:, None]
    a_dt = a * dt
    a_bar = jnp.exp(a_dt)
    b_bar = ((a_bar - 1.0) / a)[..., None] * lax.complex(b_re.astype(F32), b_im.astype(F32))
    c_mat = lax.complex(c_re.astype(F32), c_im.astype(F32))
    steps = jnp.arange(t_len + 1, dtype=F32)
    a_pow = jnp.exp(a_dt[:, None, :] * steps[None, :, None])

    taps = jnp.einsum('gop,gtp,gpi->gtoi', c_mat, a_pow[:, :t_len], b_bar,
                      precision=lax.Precision.HIGHEST).real
    taps = jnp.concatenate([taps, jnp.zeros_like(taps[:, :1])], axis=1)
    s_idx = jnp.arange(t_len)[:, None]
    t_idx = jnp.arange(t_len)[None, :]
    lag = jnp.where(t_idx >= s_idx, t_idx - s_idx, t_len)
    conv = taps[:, lag]
    conv = conv.transpose(0, 1, 4, 2, 3).reshape(n_g, t_len * S5_GROUP, t_len * S5_GROUP)

    inp_c = a_pow[:, t_len - 1::-1][:, :t_len, None, :] * b_bar.transpose(0, 2, 1)[:, None]
    inp = jnp.concatenate([inp_c.real, inp_c.imag], axis=-1)
    inp = inp.reshape(n_g, t_len * S5_GROUP, 2 * n_p)

    out_c = c_mat[:, None] * a_pow[:, 1:, None, :]
    outp = jnp.concatenate([out_c.real, -out_c.imag], axis=-1)
    outp = outp.reshape(n_g, t_len * S5_GROUP, 2 * n_p).transpose(0, 2, 1)
    outp = jnp.concatenate([outp, outp], axis=1)
    return conv, inp, outp, a_dt


def _s5_decay(a_dt, n_levels):
    jumps = (S5_CHUNK * 2 ** jnp.arange(n_levels)).astype(F32)
    d = jnp.exp(a_dt[:, None, :] * jumps[None, :, None])
    same = jnp.concatenate([d.real, d.real], axis=-1)
    cross = jnp.concatenate([-d.imag, d.imag], axis=-1)
    return jnp.stack([same, cross], axis=2)


def _s5_scan(u_g, conv, inp, outp, decay, n_b, groups_per_step):
    n_g, n_rows, width = u_g.shape
    n_levels = decay.shape[1]
    state = inp.shape[2]
    gps = groups_per_step

    def shift_rows(x, n):
        row = lax.broadcasted_iota(jnp.int32, x.shape, 0)
        return jnp.where(row >= n, pltpu.roll(x, n, 0), 0.0)

    def body(u_ref, conv_ref, inp_ref, outp_ref, decay_ref, y_ref):
        for g in range(gps):
            u = u_ref[g]
            hs = jnp.dot(u, inp_ref[g], preferred_element_type=F32)
            for lvl in range(n_levels):
                prev = shift_rows(hs, n_b * 2 ** lvl)
                same = decay_ref[g, lvl, 0:1, :]
                cross = decay_ref[g, lvl, 1:2, :]
                hs = hs + same * prev + cross * pltpu.roll(prev, state // 2, 1)
            h_in = shift_rows(hs, n_b)
            hi = h_in.astype(BF16)
            lo = (h_in - hi.astype(F32)).astype(BF16)
            y = jnp.dot(u, conv_ref[g], preferred_element_type=F32)
            y = y + jnp.dot(jnp.concatenate([hi, lo], axis=1), outp_ref[g],
                            preferred_element_type=F32)
            y_ref[g] = y

    return pl.pallas_call(
        body,
        grid=(n_g // gps,),
        in_specs=[
            pl.BlockSpec((gps, n_rows, width), lambda i: (i, 0, 0)),
            pl.BlockSpec((gps, width, width), lambda i: (i, 0, 0)),
            pl.BlockSpec((gps, width, state), lambda i: (i, 0, 0)),
            pl.BlockSpec((gps, 2 * state, width), lambda i: (i, 0, 0)),
            pl.BlockSpec((gps, n_levels, 2, state), lambda i: (i, 0, 0, 0)),
        ],
        out_specs=pl.BlockSpec((gps, n_rows, width), lambda i: (i, 0, 0)),
        out_shape=jax.ShapeDtypeStruct((n_g, n_rows, width), F32),
        compiler_params=_compiler_params(("parallel",)),
    )(u_g, conv, inp, outp, decay)


def _s5_layer(h, gain, w_in, a_re, a_im, log_dt, b_re, b_im, c_re, c_im, d_skip,
              w_glu, n_b, seq, tiles):
    d_model = h.shape[1]
    tm, tn = tiles
    n_g = d_model // S5_GROUP
    n_chunks = seq // S5_CHUNK
    u = _norm_proj(h, gain, w_in, F32, tm, tn)
    conv, inp, outp, a_dt = _s5_matrices(a_re, a_im, log_dt, b_re, b_im, c_re, c_im)
    n_levels = max(1, math.ceil(math.log2(n_chunks)))
    decay = _s5_decay(a_dt, n_levels)
    u_g = u.reshape(n_b, n_chunks, S5_CHUNK, n_g, S5_GROUP).transpose(3, 1, 0, 2, 4)
    u_g = u_g.reshape(n_g, n_chunks * n_b, S5_CHUNK * S5_GROUP).astype(BF16)
    y_g = _s5_scan(u_g, conv.astype(BF16), inp.astype(BF16), outp.astype(BF16),
                   decay, n_b, 4)
    y = y_g.reshape(n_g, n_chunks, n_b, S5_CHUNK, S5_GROUP).transpose(2, 1, 3, 0, 4)
    y = y.reshape(n_b * seq, d_model)

    def prologue(a_tiles, consts):
        z = a_tiles[0] + consts[0] * a_tiles[1]
        return [jax.nn.gelu(z).astype(BF16)]

    def epilogue(accs, ex):
        return ex[0] + accs[0] * _sigmoid(accs[1])

    tm_glu = min(tm, 512)
    return _fused_matmul([y, u], [d_skip.astype(F32).reshape(1, d_model)], prologue,
                         [d_model], [(0, w_glu, 0), (0, w_glu, d_model // tn)],
                         [(h, 'tile')], epilogue, F32, d_model, tm_glu, tn)


def _tiles(m_rows):
    tm = min(1024, m_rows)
    return tm, 512


@jax.jit
def _trunk(x, p, norm_g, final_g, w_ffn_in, w_ffn_out, w_ple_gate, w_ple_proj,
           fox_w_in, fox_b_f, fox_w_o, moba_w_qkv, moba_w_o,
           s5_w_in, s5_a_re, s5_a_im, s5_log_dt, s5_b_re, s5_b_im,
           s5_c_re, s5_c_im, s5_d, s5_w_glu):
    n_b, seq, d_model = x.shape
    depth = norm_g.shape[0]
    m_rows = n_b * seq
    tiles = _tiles(m_rows)
    tm, tn = tiles
    tm_half = min(512, m_rows)
    h = x.reshape(m_rows, d_model)
    for i in range(depth):
        g = norm_g[i]
        h = _ffn(h, g[0], w_ffn_in[i, 0].astype(BF16), w_ffn_out[i, 0].astype(BF16),
                 tm, tn, tm_half, tn)
        kind, j = i % 3, i // 3
        if kind == 0:
            h = _fox_layer(h, g[1], fox_w_in[j].astype(BF16), fox_b_f[j],
                           fox_w_o[j].astype(BF16), n_b, seq, tiles)
        elif kind == 1:
            h = _moba_layer(h, g[1], moba_w_qkv[j].astype(BF16),
                            moba_w_o[j].astype(BF16), n_b, seq, tiles)
        else:
            h = _s5_layer(h, g[1], s5_w_in[j].astype(BF16), s5_a_re[j], s5_a_im[j],
                          s5_log_dt[j], s5_b_re[j], s5_b_im[j], s5_c_re[j], s5_c_im[j],
                          s5_d[j], s5_w_glu[j].astype(BF16), n_b, seq, tiles)
        h = _ffn(h, g[2], w_ffn_in[i, 1].astype(BF16), w_ffn_out[i, 1].astype(BF16),
                 tm, tn, tm_half, tn)
        h = _ple(h, g[3], p[i].reshape(m_rows, -1), w_ple_gate[i].astype(BF16),
                 w_ple_proj[i].astype(BF16), tm_half, tn)
    return _final_norm(h, final_g, tm_half).reshape(n_b, seq, d_model)


def kernel(x, p, norm_g, final_g, w_ffn_in, w_ffn_out, w_ple_gate, w_ple_proj, fox_w_in, fox_b_f, fox_w_o, moba_w_qkv, moba_w_o, s5_w_in, s5_a_re, s5_a_im, s5_log_dt, s5_b_re, s5_b_im, s5_c_re, s5_c_im, s5_d, s5_w_glu):
    return _trunk(x, p, norm_g, final_g, w_ffn_in, w_ffn_out, w_ple_gate, w_ple_proj,
                  fox_w_in, fox_b_f, fox_w_o, moba_w_qkv, moba_w_o,
                  s5_w_in, s5_a_re, s5_a_im, s5_log_dt, s5_b_re, s5_b_im,
                  s5_c_re, s5_c_im, s5_d, s5_w_glu)
```

```python
import functools
import math

import jax
import jax.numpy as jnp
from jax import lax
from jax.experimental import pallas as pl
from jax.experimental.pallas import tpu as pltpu

F32 = jnp.float32
BF16 = jnp.bfloat16

N_HEADS = 16
HEAD_DIM = 128
LANES = 128
SUBLANES = 8
ATTN_BLOCK = 256
MOBA_BLOCK = 256
MOBA_TOP_K = 3
S5_GROUP = 16
S5_STATE = 64
S5_CHUNK = 16
RMS_EPS = 1e-6
MASK_VALUE = -1e30
LOG2E = math.log2(math.e)
Q_PRESCALE = HEAD_DIM ** -0.5 * LOG2E

V7X_VMEM_BYTES = 64 * 1024 * 1024
VMEM_LIMIT_BYTES = V7X_VMEM_BYTES - 8 * 1024 * 1024
PROLOGUE_ROWS = 256


def _compiler_params(semantics):
    return pltpu.CompilerParams(dimension_semantics=semantics,
                                vmem_limit_bytes=VMEM_LIMIT_BYTES)


def _sigmoid(x):
    return 1.0 / (1.0 + jnp.exp(-x))


def _rms_rows(x, g):
    ms = jnp.mean(x * x, axis=-1, keepdims=True)
    return x * lax.rsqrt(ms + RMS_EPS) * g


def _prefixed_spec(prefix, block, index_map):
    return pl.BlockSpec((None,) * len(prefix) + block,
                        lambda *ids: prefix + index_map(*ids))


def _fused_matmul(a_list, row_consts, prologue, lhs_widths, terms, extras,
                  epilogue, out_dtype, n_out, tm, tn):
    m_rows = a_list[0][0].shape[-2]
    grid = (m_rows // tm, n_out // tn)
    n_a, n_c, n_t, n_e = len(a_list), len(row_consts), len(terms), len(extras)

    def body(*refs):
        a_refs = refs[:n_a]
        c_refs = refs[n_a:n_a + n_c]
        w_refs = refs[n_a + n_c:n_a + n_c + n_t]
        e_refs = refs[n_a + n_c + n_t:n_a + n_c + n_t + n_e]
        o_ref = refs[n_a + n_c + n_t + n_e]
        lhs_refs = refs[n_a + n_c + n_t + n_e + 1:]
        if prologue is not None:
            @pl.when(pl.program_id(1) == 0)
            def _():
                consts = [c[...] for c in c_refs]

                def rows_step(r, carry):
                    rows = pl.ds(pl.multiple_of(r * PROLOGUE_ROWS, PROLOGUE_ROWS),
                                 PROLOGUE_ROWS)
                    outs = prologue([a[rows, :] for a in a_refs], consts)
                    for dst, val in zip(lhs_refs, outs):
                        dst[rows, :] = val
                    return carry

                lax.fori_loop(0, tm // PROLOGUE_ROWS, rows_step, 0)
        else:
            lhs_refs = a_refs
        accs = [jnp.dot(lhs_refs[li][...], w[...].astype(BF16),
                        preferred_element_type=F32)
                for (li, _, _, _), w in zip(terms, w_refs)]
        o_ref[...] = epilogue(accs, [e[...] for e in e_refs]).astype(out_dtype)

    in_specs = [_prefixed_spec(pre, (tm, a.shape[-1]), lambda i, j: (i, 0))
                for a, pre in a_list]
    in_specs += [pl.BlockSpec((1, c.shape[1]), lambda i, j: (0, 0)) for c in row_consts]
    for _, w, pre, off in terms:
        in_specs.append(_prefixed_spec(
            pre, (w.shape[-2], tn), functools.partial(lambda i, j, o: (0, j + o), o=off)))
    for arr, kind in extras:
        if kind == 'tile':
            in_specs.append(pl.BlockSpec((tm, tn), lambda i, j: (i, j)))
        else:
            in_specs.append(pl.BlockSpec((1, tn), lambda i, j: (0, j)))
    scratch = []
    if prologue is not None:
        scratch = [pltpu.VMEM((tm, k), BF16) for k in lhs_widths]
    return pl.pallas_call(
        body,
        grid=grid,
        in_specs=in_specs,
        out_specs=pl.BlockSpec((tm, tn), lambda i, j: (i, j)),
        out_shape=jax.ShapeDtypeStruct((m_rows, n_out), out_dtype),
        scratch_shapes=scratch,
        compiler_params=_compiler_params(("parallel", "arbitrary")),
    )(*[a for a, _ in a_list], *row_consts, *[w for _, w, _, _ in terms],
      *[e for e, _ in extras])


def _norm_prologue(a_tiles, consts):
    return [_rms_rows(a_tiles[0], consts[0]).astype(BF16)]


def _ffn(h, gain, w_in, w_out, tn_in, tn_out, tm):
    d_model = h.shape[1]
    d_ff = w_out[0].shape[-2]

    def act_epilogue(accs, _):
        gate, up = accs
        return gate * _sigmoid(gate) * up

    act = _fused_matmul([(h, ())], [gain.reshape(1, d_model)], _norm_prologue, [d_model],
                        [(0, *w_in, 0), (0, *w_in, d_ff // tn_in)], [],
                        act_epilogue, BF16, d_ff, tm, tn_in)

    def res_epilogue(accs, ex):
        return ex[0] + 0.5 * accs[0]

    return _fused_matmul([(act, ())], [], None, [], [(0, *w_out, 0)], [(h, 'tile')],
                         res_epilogue, F32, d_model, tm, tn_out)


def _ple(h, gain, p_i, w_gate, w_proj, tm, tn):
    d_model = h.shape[1]

    def prologue(a_tiles, consts):
        return [_rms_rows(a_tiles[0], consts[0]).astype(BF16), a_tiles[1].astype(BF16)]

    def epilogue(accs, ex):
        return ex[0] + _sigmoid(accs[0]) * accs[1]

    return _fused_matmul([(h, ()), p_i], [gain.reshape(1, d_model)], prologue,
                         [d_model, p_i[0].shape[-1]],
                         [(0, *w_gate, 0), (1, *w_proj, 0)], [(h, 'tile')],
                         epilogue, F32, d_model, tm, tn)


def _norm_proj(h, gain, w, out_dtype, tm, tn, n_out, q_blocks=0):
    d_model = h.shape[1]

    def epilogue(accs, _):
        if not q_blocks:
            return accs[0]
        return accs[0] * jnp.where(pl.program_id(1) < q_blocks, Q_PRESCALE, 1.0)

    return _fused_matmul([(h, ())], [gain.reshape(1, d_model)], _norm_prologue, [d_model],
                         [(0, *w, 0)], [], epilogue, out_dtype, n_out, tm, tn)


def _out_proj(a, w, h, tm, tn):
    return _fused_matmul([(a, ())], [], None, [], [(0, *w, 0)], [(h, 'tile')],
                         lambda accs, ex: ex[0] + accs[0], F32, h.shape[1], tm, tn)


def _final_norm(h, gain, tm):
    m_rows, d_model = h.shape

    def body(h_ref, g_ref, o_ref):
        o_ref[...] = _rms_rows(h_ref[...], g_ref[...])

    return pl.pallas_call(
        body,
        grid=(m_rows // tm,),
        in_specs=[pl.BlockSpec((tm, d_model), lambda i: (i, 0)),
                  pl.BlockSpec((1, d_model), lambda i: (0, 0))],
        out_specs=pl.BlockSpec((tm, d_model), lambda i: (i, 0)),
        out_shape=jax.ShapeDtypeStruct((m_rows, d_model), F32),
        compiler_params=_compiler_params(("parallel",)),
    )(h, gain.reshape(1, d_model))


def _qk(q, k):
    return lax.dot_general(q, k, (((1,), (1,)), ((), ())), preferred_element_type=F32)


def _bf16_pieces(x):
    hi = x.astype(BF16).astype(F32)
    rest = x - hi
    mid = rest.astype(BF16).astype(F32)
    return hi, mid, rest - mid


def _lane_table(lane, entries, default=0.0):
    out = default
    for idx, val in reversed(list(entries.items())):
        out = jnp.where(lane == idx, val, out)
    return out


def _causal_softmax_pv(q_full, kaug_ref, v_ref, n_blocks):
    blk = ATTN_BLOCK
    n_keys = n_blocks * blk
    s = _qk(q_full, kaug_ref[0:n_keys, :])
    row = lax.broadcasted_iota(jnp.int32, (blk, blk), 0)
    col = lax.broadcasted_iota(jnp.int32, (blk, blk), 1)
    diag = jnp.where(row >= col, s[:, n_keys - blk:], MASK_VALUE)
    parts = [s[:, i * LANES:(i + 1) * LANES] for i in range((n_keys - blk) // LANES)]
    parts += [diag[:, :LANES], diag[:, LANES:]]
    m = jnp.max(functools.reduce(jnp.maximum, parts), axis=1, keepdims=True)
    probs = [jnp.exp2(part - m) for part in parts]
    denom = jnp.sum(functools.reduce(jnp.add, probs), axis=1, keepdims=True)
    p = jnp.concatenate([pr.astype(BF16) for pr in probs], axis=1)
    acc = jnp.dot(p, v_ref[0, 0:n_keys, :], preferred_element_type=F32)
    return acc / denom


def _log_forget_cumsum(logf):
    n_b, seq, lanes = logf.shape

    def body(x_ref, o_ref):
        x = x_ref[0]
        row = lax.broadcasted_iota(jnp.int32, (seq, lanes), 0)
        shift = 1
        while shift < seq:
            x = x + jnp.where(row >= shift, pltpu.roll(x, shift, 0), 0.0)
            shift *= 2
        o_ref[0] = x

    return pl.pallas_call(
        body,
        grid=(n_b,),
        in_specs=[pl.BlockSpec((1, seq, lanes), lambda b: (b, 0, 0))],
        out_specs=pl.BlockSpec((1, seq, lanes), lambda b: (b, 0, 0)),
        out_shape=jax.ShapeDtypeStruct((n_b, seq, lanes), F32),
        compiler_params=_compiler_params(("parallel",)),
    )(logf)


def _fox_attention(qkv, c):
    n_b, seq, _ = qkv.shape
    blk = ATTN_BLOCK
    n_blk = seq // blk

    def body(q_ref, k_ref, v_ref, c_ref, o_ref, kaug_ref, qaug_ref):
        h = pl.program_id(1)

        def rows_step(r, carry):
            rows = pl.ds(pl.multiple_of(r * blk, blk), blk)
            lane = lax.broadcasted_iota(jnp.int32, (blk, LANES), 1)
            c_head = jnp.sum(jnp.where(lane == h, c_ref[0, rows, :], 0.0),
                             axis=1, keepdims=True) * LOG2E
            hi, mid, lo = _bf16_pieces(c_head)
            q_aug = _lane_table(lane, {0: hi, 1: mid, 2: lo, 3: 1.0, 4: 1.0, 5: 1.0})
            k_aug = _lane_table(lane, {0: 1.0, 1: 1.0, 2: 1.0, 3: -hi, 4: -mid, 5: -lo})
            qaug_ref[rows, :] = q_aug.astype(BF16)
            kaug_ref[rows, :LANES] = k_ref[0, rows, :]
            kaug_ref[rows, LANES:] = k_aug.astype(BF16)
            return carry

        lax.fori_loop(0, n_blk, rows_step, 0)
        for qi in range(n_blk):
            rows = slice(qi * blk, (qi + 1) * blk)
            q_full = jnp.concatenate([q_ref[0, rows, :], qaug_ref[rows, :]], axis=1)
            o_ref[0, rows, :] = _causal_softmax_pv(q_full, kaug_ref, v_ref,
                                                   qi + 1).astype(BF16)

    return pl.pallas_call(
        body,
        grid=(n_b, N_HEADS),
        in_specs=[
            pl.BlockSpec((1, seq, HEAD_DIM), lambda b, h: (b, 0, h)),
            pl.BlockSpec((1, seq, HEAD_DIM), lambda b, h: (b, 0, N_HEADS + h)),
            pl.BlockSpec((1, seq, HEAD_DIM), lambda b, h: (b, 0, 2 * N_HEADS + h)),
            pl.BlockSpec((1, seq, LANES), lambda b, h: (b, 0, 0)),
        ],
        out_specs=pl.BlockSpec((1, seq, HEAD_DIM), lambda b, h: (b, 0, h)),
        out_shape=jax.ShapeDtypeStruct((n_b, seq, N_HEADS * HEAD_DIM), BF16),
        scratch_shapes=[pltpu.VMEM((seq, 2 * LANES), BF16),
                        pltpu.VMEM((seq, LANES), BF16)],
        compiler_params=_compiler_params(("parallel", "arbitrary")),
    )(qkv, qkv, qkv, c)


def _fox_layer(h, gain, w_in, b_f, w_o, n_b, seq, tm, tn):
    d_model = h.shape[1]
    qkv = _norm_proj(h, gain, w_in, BF16, tm, tn, 3 * d_model, q_blocks=d_model // tn)
    w_f = jnp.pad(w_in[0][w_in[1]][:, 3 * d_model:], ((0, 0), (0, LANES - N_HEADS)))
    b_row = jnp.pad(b_f, (0, LANES - N_HEADS)).reshape(1, LANES)

    def logsig_epilogue(accs, ex):
        z = accs[0] + ex[0]
        return jnp.minimum(z, 0.0) - jnp.log(1.0 + jnp.exp(-jnp.abs(z)))

    logf = _fused_matmul([(h, ())], [gain.reshape(1, d_model)], _norm_prologue, [d_model],
                         [(0, w_f, (), 0)], [(b_row, 'row')], logsig_epilogue, F32,
                         LANES, tm, LANES)
    c = _log_forget_cumsum(logf.reshape(n_b, seq, LANES))
    attn = _fox_attention(qkv.reshape(n_b, seq, 3 * d_model), c)
    return _out_proj(attn.reshape(n_b * seq, d_model), w_o, h, tm, tn)


def _moba_attention(qkv, slopes):
    n_b, seq, _ = qkv.shape
    blk = MOBA_BLOCK
    assert blk == ATTN_BLOCK
    n_blk = seq // blk
    assert n_blk <= SUBLANES
    k_lanes = [SUBLANES, SUBLANES + 1, SUBLANES + 2]
    q_lanes = [SUBLANES + 3, SUBLANES + 4, SUBLANES + 5]

    def body(slopes_ref, q_ref, k_ref, v_ref, o_ref, kaug_ref, qaug_ref, kmean_ref):
        h = pl.program_id(1)
        slope2 = slopes_ref[h] * LOG2E

        kmean_ref[...] = jnp.zeros_like(kmean_ref)
        for n in range(n_blk):
            rows = slice(n * blk, (n + 1) * blk)
            kb = k_ref[0, rows, :]
            kmean_ref[n:n + 1, :] = jnp.mean(kb.astype(F32), axis=0, keepdims=True)
            lane = lax.broadcasted_iota(jnp.int32, (blk, LANES), 1)
            pos = (lax.broadcasted_iota(jnp.int32, (blk, 1), 0) + n * blk).astype(F32)
            hi, mid, lo = _bf16_pieces(slope2 * pos)
            entries = {n: 1.0, k_lanes[0]: hi, k_lanes[1]: mid, k_lanes[2]: lo,
                       q_lanes[0]: 1.0, q_lanes[1]: 1.0, q_lanes[2]: 1.0}
            kaug_ref[rows, :LANES] = kb
            kaug_ref[rows, LANES:] = _lane_table(lane, entries).astype(BF16)

        gate = _qk(kmean_ref[...].astype(BF16), q_ref[0])[:SUBLANES]
        key_blk = lax.broadcasted_iota(jnp.int32, (SUBLANES, seq), 0)
        q_pos = lax.broadcasted_iota(jnp.int32, (SUBLANES, seq), 1)
        q_blk = q_pos // blk
        beaten = jnp.zeros((SUBLANES, seq), F32)
        for n2 in range(n_blk):
            rival = gate[n2:n2 + 1, :]
            wins = jnp.where(key_blk > n2, jnp.where(rival >= gate, 1.0, 0.0),
                             jnp.where(rival > gate, 1.0, 0.0))
            beaten = beaten + jnp.where(n2 < q_blk, wins, 0.0)
        bias = jnp.where(key_blk < q_blk,
                         jnp.where(beaten < MOBA_TOP_K, 0.0, MASK_VALUE),
                         jnp.where(key_blk == q_blk, 0.0, MASK_VALUE))
        hi, mid, lo = _bf16_pieces(-slope2 * q_pos.astype(F32))
        pieces = jnp.where(key_blk < 3, 1.0,
                           jnp.where(key_blk == 3, hi,
                                     jnp.where(key_blk == 4, mid,
                                               jnp.where(key_blk == 5, lo, 0.0))))
        aug_t = jnp.concatenate(
            [bias, pieces, jnp.zeros((LANES - 2 * SUBLANES, seq), F32)], axis=0)
        for qi in range(n_blk):
            rows = slice(qi * blk, (qi + 1) * blk)
            qaug_ref[rows, :] = aug_t[:, rows].T.astype(BF16)
        for qi in range(n_blk):
            rows = slice(qi * blk, (qi + 1) * blk)
            q_full = jnp.concatenate([q_ref[0, rows, :], qaug_ref[rows, :]], axis=1)
            o_ref[0, rows, :] = _causal_softmax_pv(q_full, kaug_ref, v_ref,
                                                   qi + 1).astype(BF16)

    return pl.pallas_call(
        body,
        grid=(n_b, N_HEADS),
        in_specs=[
            pl.BlockSpec(memory_space=pltpu.SMEM),
            pl.BlockSpec((1, seq, HEAD_DIM), lambda b, h: (b, 0, h)),
            pl.BlockSpec((1, seq, HEAD_DIM), lambda b, h: (b, 0, N_HEADS + h)),
            pl.BlockSpec((1, seq, HEAD_DIM), lambda b, h: (b, 0, 2 * N_HEADS + h)),
        ],
        out_specs=pl.BlockSpec((1, seq, HEAD_DIM), lambda b, h: (b, 0, h)),
        out_shape=jax.ShapeDtypeStruct((n_b, seq, N_HEADS * HEAD_DIM), BF16),
        scratch_shapes=[pltpu.VMEM((seq, 2 * LANES), BF16),
                        pltpu.VMEM((seq, LANES), BF16),
                        pltpu.VMEM((2 * SUBLANES, HEAD_DIM), F32)],
        compiler_params=_compiler_params(("parallel", "arbitrary")),
    )(slopes, qkv, qkv, qkv)


def _moba_layer(h, gain, w_qkv, w_o, n_b, seq, tm, tn):
    d_model = h.shape[1]
    qkv = _norm_proj(h, gain, w_qkv, BF16, tm, tn, 3 * d_model, q_blocks=d_model // tn)
    slopes = 2.0 ** (-8.0 * jnp.arange(1, N_HEADS + 1, dtype=F32) / N_HEADS)
    attn = _moba_attention(qkv.reshape(n_b, seq, 3 * d_model), slopes)
    return _out_proj(attn.reshape(n_b * seq, d_model), w_o, h, tm, tn)


def _s5_matrices(a_re, a_im, log_dt, b_re, b_im, c_re, c_im):
    n_g, n_p = a_re.shape
    t_len = S5_CHUNK
    a = lax.complex(a_re.astype(F32), a_im.astype(F32))
    dt = jnp.exp(log_dt.astype(F32))[:, None]
    a_dt = a * dt
    a_bar = jnp.exp(a_dt)
    b_bar = ((a_bar - 1.0) / a)[..., None] * lax.complex(b_re.astype(F32), b_im.astype(F32))
    c_mat = lax.complex(c_re.astype(F32), c_im.astype(F32))
    steps = jnp.arange(t_len + 1, dtype=F32)
    a_pow = jnp.exp(a_dt[:, None, :] * steps[None, :, None])

    taps = jnp.einsum('gop,gtp,gpi->gtoi', c_mat, a_pow[:, :t_len], b_bar,
                      precision=lax.Precision.HIGHEST).real
    taps = jnp.concatenate([taps, jnp.zeros_like(taps[:, :1])], axis=1)
    s_idx = jnp.arange(t_len)[:, None]
    t_idx = jnp.arange(t_len)[None, :]
    lag = jnp.where(t_idx >= s_idx, t_idx - s_idx, t_len)
    conv = taps[:, lag]
    conv = conv.transpose(0, 1, 4, 2, 3).reshape(n_g, t_len * S5_GROUP, t_len * S5_GROUP)

    inp_c = a_pow[:, t_len - 1::-1][:, :t_len, None, :] * b_bar.transpose(0, 2, 1)[:, None]
    inp = jnp.concatenate([inp_c.real, inp_c.imag], axis=-1)
    inp = inp.reshape(n_g, t_len * S5_GROUP, 2 * n_p)

    out_c = c_mat[:, None] * a_pow[:, 1:, None, :]
    outp = jnp.concatenate([out_c.real, -out_c.imag], axis=-1)
    outp = outp.reshape(n_g, t_len * S5_GROUP, 2 * n_p).transpose(0, 2, 1)
    outp = jnp.concatenate([outp, outp], axis=1)
    return conv, inp, outp, a_dt


def _s5_decay(a_dt, n_levels):
    jumps = (S5_CHUNK * 2 ** jnp.arange(n_levels)).astype(F32)
    d = jnp.exp(a_dt[:, None, :] * jumps[None, :, None])
    same = jnp.concatenate([d.real, d.real], axis=-1)
    cross = jnp.concatenate([-d.imag, d.imag], axis=-1)
    return jnp.stack([same, cross], axis=2)


def _s5_scan(u_g, conv, inp, outp, decay, n_b, groups_per_step):
    n_g, n_rows, width = u_g.shape
    n_levels = decay.shape[1]
    state = inp.shape[2]
    gps = groups_per_step

    def shift_rows(x, n):
        row = lax.broadcasted_iota(jnp.int32, x.shape, 0)
        return jnp.where(row >= n, pltpu.roll(x, n, 0), 0.0)

    def body(u_ref, conv_ref, inp_ref, outp_ref, decay_ref, y_ref):
        for g in range(gps):
            u = u_ref[g]
            hs = jnp.dot(u, inp_ref[g], preferred_element_type=F32)
            for lvl in range(n_levels):
                prev = shift_rows(hs, n_b * 2 ** lvl)
                same = decay_ref[g, lvl, 0:1, :]
                cross = decay_ref[g, lvl, 1:2, :]
                hs = hs + same * prev + cross * pltpu.roll(prev, state // 2, 1)
            h_in = shift_rows(hs, n_b)
            hi = h_in.astype(BF16)
            lo = (h_in - hi.astype(F32)).astype(BF16)
            y = jnp.dot(u, conv_ref[g], preferred_element_type=F32)
            y = y + jnp.dot(jnp.concatenate([hi, lo], axis=1), outp_ref[g],
                            preferred_element_type=F32)
            y_ref[g] = y

    return pl.pallas_call(
        body,
        grid=(n_g // gps,),
        in_specs=[
            pl.BlockSpec((gps, n_rows, width), lambda i: (i, 0, 0)),
            pl.BlockSpec((gps, width, width), lambda i: (i, 0, 0)),
            pl.BlockSpec((gps, width, state), lambda i: (i, 0, 0)),
            pl.BlockSpec((gps, 2 * state, width), lambda i: (i, 0, 0)),
            pl.BlockSpec((gps, n_levels, 2, state), lambda i: (i, 0, 0, 0)),
        ],
        out_specs=pl.BlockSpec((gps, n_rows, width), lambda i: (i, 0, 0)),
        out_shape=jax.ShapeDtypeStruct((n_g, n_rows, width), F32),
        compiler_params=_compiler_params(("parallel",)),
    )(u_g, conv, inp, outp, decay)


def _s5_layer(h, gain, w_in, a_re, a_im, log_dt, b_re, b_im, c_re, c_im, d_skip,
              w_glu, n_b, seq, tm, tn):
    d_model = h.shape[1]
    n_g = d_model // S5_GROUP
    n_chunks = seq // S5_CHUNK
    u = _norm_proj(h, gain, w_in, F32, tm, tn, d_model)
    conv, inp, outp, a_dt = _s5_matrices(a_re, a_im, log_dt, b_re, b_im, c_re, c_im)
    n_levels = max(1, math.ceil(math.log2(n_chunks)))
    decay = _s5_decay(a_dt, n_levels)
    u_g = u.reshape(n_b, n_chunks, S5_CHUNK, n_g, S5_GROUP).transpose(3, 1, 0, 2, 4)
    u_g = u_g.reshape(n_g, n_chunks * n_b, S5_CHUNK * S5_GROUP).astype(BF16)
    y_g = _s5_scan(u_g, conv.astype(BF16), inp.astype(BF16), outp.astype(BF16),
                   decay, n_b, 4)
    y = y_g.reshape(n_g, n_chunks, n_b, S5_CHUNK, S5_GROUP).transpose(2, 1, 3, 0, 4)
    y = y.reshape(n_b * seq, d_model)

    def prologue(a_tiles, consts):
        z = a_tiles[0] + consts[0] * a_tiles[1]
        return [jax.nn.gelu(z).astype(BF16)]

    def epilogue(accs, ex):
        return ex[0] + accs[0] * _sigmoid(accs[1])

    tm_glu = min(tm, 512)
    return _fused_matmul([(y, ()), (u, ())], [d_skip.astype(F32).reshape(1, d_model)],
                         prologue, [d_model],
                         [(0, *w_glu, 0), (0, *w_glu, d_model // tn)],
                         [(h, 'tile')], epilogue, F32, d_model, tm_glu, tn)


@jax.jit
def _trunk(x, p, norm_g, final_g, w_ffn_in, w_ffn_out, w_ple_gate, w_ple_proj,
           fox_w_in, fox_b_f, fox_w_o, moba_w_qkv, moba_w_o,
           s5_w_in, s5_a_re, s5_a_im, s5_log_dt, s5_b_re, s5_b_im,
           s5_c_re, s5_c_im, s5_d, s5_w_glu):
    n_b, seq, d_model = x.shape
    depth = norm_g.shape[0]
    m_rows = n_b * seq
    tm = min(1024, m_rows)
    tn = 512
    tn_ffn_out = 256
    h = x.reshape(m_rows, d_model)
    p_rows = p.reshape(depth, m_rows, p.shape[-1])
    for i in range(depth):
        g = norm_g[i]
        h = _ffn(h, g[0], (w_ffn_in, (i, 0)), (w_ffn_out, (i, 0)), tn, tn_ffn_out, tm)
        kind, j = i % 3, i // 3
        if kind == 0:
            h = _fox_layer(h, g[1], (fox_w_in, (j,)), fox_b_f[j], (fox_w_o, (j,)),
                           n_b, seq, tm, tn)
        elif kind == 1:
            h = _moba_layer(h, g[1], (moba_w_qkv, (j,)), (moba_w_o, (j,)), n_b, seq, tm, tn)
        else:
            h = _s5_layer(h, g[1], (s5_w_in, (j,)), s5_a_re[j], s5_a_im[j],
                          s5_log_dt[j], s5_b_re[j], s5_b_im[j], s5_c_re[j], s5_c_im[j],
                          s5_d[j], (s5_w_glu, (j,)), n_b, seq, tm, tn)
        h = _ffn(h, g[2], (w_ffn_in, (i, 1)), (w_ffn_out, (i, 1)), tn, tn_ffn_out, tm)
        h = _ple(h, g[3], (p_rows, (i,)), (w_ple_gate, (i,)), (w_ple_proj, (i,)),
                 min(tm, 512), tn)
    return _final_norm(h, final_g, min(tm, 512)).reshape(n_b, seq, d_model)


def kernel(x, p, norm_g, final_g, w_ffn_in, w_ffn_out, w_ple_gate, w_ple_proj, fox_w_in, fox_b_f, fox_w_o, moba_w_qkv, moba_w_o, s5_w_in, s5_a_re, s5_a_im, s5_log_dt, s5_b_re, s5_b_im, s5_c_re, s5_c_im, s5_d, s5_w_glu):
    return _trunk(x, p, norm_g, final_g, w_ffn_in, w_ffn_out, w_ple_gate, w_ple_proj,
                  fox_w_in, fox_b_f, fox_w_o, moba_w_qkv, moba_w_o,
                  s5_w_in, s5_a_re, s5_a_im, s5_log_dt, s5_b_re, s5_b_im,
                  s5_c_re, s5_c_im, s5_d, s5_w_glu)
```

```python
import functools
import math

import jax
import jax.numpy as jnp
from jax import lax
from jax.experimental import pallas as pl
from jax.experimental.pallas import tpu as pltpu

F32 = jnp.float32
BF16 = jnp.bfloat16

N_HEADS = 16
HEAD_DIM = 128
LANES = 128
SUBLANES = 8
ATTN_BLOCK = 256
MOBA_BLOCK = 256
MOBA_TOP_K = 3
S5_GROUP = 16
S5_STATE = 64
S5_CHUNK = 16
RMS_EPS = 1e-6
MASK_VALUE = -1e30
LOG2E = math.log2(math.e)
Q_PRESCALE = HEAD_DIM ** -0.5 * LOG2E

V7X_VMEM_BYTES = 64 * 1024 * 1024
VMEM_LIMIT_BYTES = V7X_VMEM_BYTES - 8 * 1024 * 1024
PROLOGUE_ROWS = 256


def _compiler_params(semantics):
    return pltpu.CompilerParams(dimension_semantics=semantics,
                                vmem_limit_bytes=VMEM_LIMIT_BYTES)


def _sigmoid(x):
    return 1.0 / (1.0 + jnp.exp(-x))


def _rms_rows(x, g):
    ms = jnp.mean(x * x, axis=-1, keepdims=True)
    return x * lax.rsqrt(ms + RMS_EPS) * g


def _prefixed_spec(prefix, block, index_map):
    return pl.BlockSpec((None,) * len(prefix) + block,
                        lambda *ids: prefix + index_map(*ids))


def _fused_matmul(a_list, row_consts, prologue, lhs_widths, terms, extras,
                  epilogue, out_dtype, n_out, tm, tn, valid_cols=None):
    m_rows = a_list[0][0].shape[-2]
    grid = (m_rows // tm, n_out // tn)
    n_a, n_c, n_t, n_e = len(a_list), len(row_consts), len(terms), len(extras)

    def body(*refs):
        a_refs = refs[:n_a]
        c_refs = refs[n_a:n_a + n_c]
        w_refs = refs[n_a + n_c:n_a + n_c + n_t]
        e_refs = refs[n_a + n_c + n_t:n_a + n_c + n_t + n_e]
        o_ref = refs[n_a + n_c + n_t + n_e]
        lhs_refs = refs[n_a + n_c + n_t + n_e + 1:]
        if prologue is not None:
            @pl.when(pl.program_id(1) == 0)
            def _():
                consts = [c[...] for c in c_refs]

                def rows_step(r, carry):
                    rows = pl.ds(pl.multiple_of(r * PROLOGUE_ROWS, PROLOGUE_ROWS),
                                 PROLOGUE_ROWS)
                    outs = prologue([a[rows, :] for a in a_refs], consts)
                    for dst, val in zip(lhs_refs, outs):
                        dst[rows, :] = val
                    return carry

                lax.fori_loop(0, tm // PROLOGUE_ROWS, rows_step, 0)
        else:
            lhs_refs = a_refs
        def weight_tile(w_ref):
            w = w_ref[...]
            if valid_cols is not None:
                lane = lax.broadcasted_iota(jnp.int32, w.shape, 1)
                w = jnp.where(lane < valid_cols, w, 0.0)
            return w.astype(BF16)

        accs = [jnp.dot(lhs_refs[li][...], weight_tile(w), preferred_element_type=F32)
                for (li, _, _, _), w in zip(terms, w_refs)]
        o_ref[...] = epilogue(accs, [e[...] for e in e_refs]).astype(out_dtype)

    in_specs = [_prefixed_spec(pre, (tm, a.shape[-1]), lambda i, j: (i, 0))
                for a, pre in a_list]
    in_specs += [pl.BlockSpec((1, c.shape[1]), lambda i, j: (0, 0)) for c in row_consts]
    for _, w, pre, off in terms:
        in_specs.append(_prefixed_spec(
            pre, (w.shape[-2], tn), functools.partial(lambda i, j, o: (0, j + o), o=off)))
    for arr, kind in extras:
        if kind == 'tile':
            in_specs.append(pl.BlockSpec((tm, tn), lambda i, j: (i, j)))
        else:
            in_specs.append(pl.BlockSpec((1, tn), lambda i, j: (0, j)))
    scratch = []
    if prologue is not None:
        scratch = [pltpu.VMEM((tm, k), BF16) for k in lhs_widths]
    return pl.pallas_call(
        body,
        grid=grid,
        in_specs=in_specs,
        out_specs=pl.BlockSpec((tm, tn), lambda i, j: (i, j)),
        out_shape=jax.ShapeDtypeStruct((m_rows, n_out), out_dtype),
        scratch_shapes=scratch,
        compiler_params=_compiler_params(("parallel", "arbitrary")),
    )(*[a for a, _ in a_list], *row_consts, *[w for _, w, _, _ in terms],
      *[e for e, _ in extras])


def _norm_prologue(a_tiles, consts):
    return [_rms_rows(a_tiles[0], consts[0]).astype(BF16)]


def _ffn(h, gain, w_in, w_out, tn_in, tn_out, tm):
    d_model = h.shape[1]
    d_ff = w_out[0].shape[-2]

    def act_epilogue(accs, _):
        gate, up = accs
        return gate * _sigmoid(gate) * up

    act = _fused_matmul([(h, ())], [gain.reshape(1, d_model)], _norm_prologue, [d_model],
                        [(0, *w_in, 0), (0, *w_in, d_ff // tn_in)], [],
                        act_epilogue, BF16, d_ff, tm, tn_in)

    def res_epilogue(accs, ex):
        return ex[0] + 0.5 * accs[0]

    return _fused_matmul([(act, ())], [], None, [], [(0, *w_out, 0)], [(h, 'tile')],
                         res_epilogue, F32, d_model, tm, tn_out)


def _ple(h, gain, p_i, w_gate, w_proj, tm, tn):
    d_model = h.shape[1]

    def prologue(a_tiles, consts):
        return [_rms_rows(a_tiles[0], consts[0]).astype(BF16), a_tiles[1].astype(BF16)]

    def epilogue(accs, ex):
        return ex[0] + _sigmoid(accs[0]) * accs[1]

    return _fused_matmul([(h, ()), p_i], [gain.reshape(1, d_model)], prologue,
                         [d_model, p_i[0].shape[-1]],
                         [(0, *w_gate, 0), (1, *w_proj, 0)], [(h, 'tile')],
                         epilogue, F32, d_model, tm, tn)


def _norm_proj(h, gain, w, out_dtype, tm, tn, n_out, q_blocks=0):
    d_model = h.shape[1]

    def epilogue(accs, _):
        if not q_blocks:
            return accs[0]
        return accs[0] * jnp.where(pl.program_id(1) < q_blocks, Q_PRESCALE, 1.0)

    return _fused_matmul([(h, ())], [gain.reshape(1, d_model)], _norm_prologue, [d_model],
                         [(0, *w, 0)], [], epilogue, out_dtype, n_out, tm, tn)


def _out_proj(a, w, h, tm, tn):
    return _fused_matmul([(a, ())], [], None, [], [(0, *w, 0)], [(h, 'tile')],
                         lambda accs, ex: ex[0] + accs[0], F32, h.shape[1], tm, tn)


def _final_norm(h, gain, tm):
    m_rows, d_model = h.shape

    def body(h_ref, g_ref, o_ref):
        o_ref[...] = _rms_rows(h_ref[...], g_ref[...])

    return pl.pallas_call(
        body,
        grid=(m_rows // tm,),
        in_specs=[pl.BlockSpec((tm, d_model), lambda i: (i, 0)),
                  pl.BlockSpec((1, d_model), lambda i: (0, 0))],
        out_specs=pl.BlockSpec((tm, d_model), lambda i: (i, 0)),
        out_shape=jax.ShapeDtypeStruct((m_rows, d_model), F32),
        compiler_params=_compiler_params(("parallel",)),
    )(h, gain.reshape(1, d_model))


def _qk(q, k):
    return lax.dot_general(q, k, (((1,), (1,)), ((), ())), preferred_element_type=F32)


def _bf16_pieces(x):
    hi = x.astype(BF16).astype(F32)
    rest = x - hi
    mid = rest.astype(BF16).astype(F32)
    return hi, mid, rest - mid


def _lane_table(lane, entries, default=0.0):
    out = default
    for idx, val in reversed(list(entries.items())):
        out = jnp.where(lane == idx, val, out)
    return out


def _causal_softmax_pv(q_full, kaug_ref, v_ref, n_blocks):
    blk = ATTN_BLOCK
    n_keys = n_blocks * blk
    s = _qk(q_full, kaug_ref[0:n_keys, :])
    row = lax.broadcasted_iota(jnp.int32, (blk, blk), 0)
    col = lax.broadcasted_iota(jnp.int32, (blk, blk), 1)
    diag = jnp.where(row >= col, s[:, n_keys - blk:], MASK_VALUE)
    parts = [s[:, i * LANES:(i + 1) * LANES] for i in range((n_keys - blk) // LANES)]
    parts += [diag[:, :LANES], diag[:, LANES:]]
    m = jnp.max(functools.reduce(jnp.maximum, parts), axis=1, keepdims=True)
    probs = [jnp.exp2(part - m) for part in parts]
    denom = jnp.sum(functools.reduce(jnp.add, probs), axis=1, keepdims=True)
    p = jnp.concatenate([pr.astype(BF16) for pr in probs], axis=1)
    acc = jnp.dot(p, v_ref[0, 0:n_keys, :], preferred_element_type=F32)
    return acc / denom


def _log_forget_cumsum(logf):
    n_b, seq, lanes = logf.shape

    def body(x_ref, o_ref):
        x = x_ref[0]
        row = lax.broadcasted_iota(jnp.int32, (seq, lanes), 0)
        shift = 1
        while shift < seq:
            x = x + jnp.where(row >= shift, pltpu.roll(x, shift, 0), 0.0)
            shift *= 2
        o_ref[0] = x

    return pl.pallas_call(
        body,
        grid=(n_b,),
        in_specs=[pl.BlockSpec((1, seq, lanes), lambda b: (b, 0, 0))],
        out_specs=pl.BlockSpec((1, seq, lanes), lambda b: (b, 0, 0)),
        out_shape=jax.ShapeDtypeStruct((n_b, seq, lanes), F32),
        compiler_params=_compiler_params(("parallel",)),
    )(logf)


def _fox_attention(qkv, c):
    n_b, seq, _ = qkv.shape
    blk = ATTN_BLOCK
    n_blk = seq // blk

    def body(q_ref, k_ref, v_ref, c_ref, o_ref, kaug_ref, qaug_ref):
        h = pl.program_id(1)

        def rows_step(r, carry):
            rows = pl.ds(pl.multiple_of(r * blk, blk), blk)
            lane = lax.broadcasted_iota(jnp.int32, (blk, LANES), 1)
            c_head = jnp.sum(jnp.where(lane == h, c_ref[0, rows, :], 0.0),
                             axis=1, keepdims=True) * LOG2E
            hi, mid, lo = _bf16_pieces(c_head)
            q_aug = _lane_table(lane, {0: hi, 1: mid, 2: lo, 3: 1.0, 4: 1.0, 5: 1.0})
            k_aug = _lane_table(lane, {0: 1.0, 1: 1.0, 2: 1.0, 3: -hi, 4: -mid, 5: -lo})
            qaug_ref[rows, :] = q_aug.astype(BF16)
            kaug_ref[rows, :LANES] = k_ref[0, rows, :]
            kaug_ref[rows, LANES:] = k_aug.astype(BF16)
            return carry

        lax.fori_loop(0, n_blk, rows_step, 0)
        for qi in range(n_blk):
            rows = slice(qi * blk, (qi + 1) * blk)
            q_full = jnp.concatenate([q_ref[0, rows, :], qaug_ref[rows, :]], axis=1)
            o_ref[0, rows, :] = _causal_softmax_pv(q_full, kaug_ref, v_ref,
                                                   qi + 1).astype(BF16)

    return pl.pallas_call(
        body,
        grid=(n_b, N_HEADS),
        in_specs=[
            pl.BlockSpec((1, seq, HEAD_DIM), lambda b, h: (b, 0, h)),
            pl.BlockSpec((1, seq, HEAD_DIM), lambda b, h: (b, 0, N_HEADS + h)),
            pl.BlockSpec((1, seq, HEAD_DIM), lambda b, h: (b, 0, 2 * N_HEADS + h)),
            pl.BlockSpec((1, seq, LANES), lambda b, h: (b, 0, 0)),
        ],
        out_specs=pl.BlockSpec((1, seq, HEAD_DIM), lambda b, h: (b, 0, h)),
        out_shape=jax.ShapeDtypeStruct((n_b, seq, N_HEADS * HEAD_DIM), BF16),
        scratch_shapes=[pltpu.VMEM((seq, 2 * LANES), BF16),
                        pltpu.VMEM((seq, LANES), BF16)],
        compiler_params=_compiler_params(("parallel", "arbitrary")),
    )(qkv, qkv, qkv, c)


def _fox_layer(h, gain, w_in, b_f, w_o, n_b, seq, tm, tn):
    d_model = h.shape[1]
    qkv = _norm_proj(h, gain, w_in, BF16, tm, tn, 3 * d_model, q_blocks=d_model // tn)
    b_row = jnp.pad(b_f, (0, LANES - N_HEADS)).reshape(1, LANES)

    def logsig_epilogue(accs, ex):
        z = accs[0] + ex[0]
        return jnp.minimum(z, 0.0) - jnp.log(1.0 + jnp.exp(-jnp.abs(z)))

    logf = _fused_matmul([(h, ())], [gain.reshape(1, d_model)], _norm_prologue, [d_model],
                         [(0, *w_in, 3 * d_model // LANES)], [(b_row, 'row')],
                         logsig_epilogue, F32, LANES, tm, LANES, valid_cols=N_HEADS)
    c = _log_forget_cumsum(logf.reshape(n_b, seq, LANES))
    attn = _fox_attention(qkv.reshape(n_b, seq, 3 * d_model), c)
    return _out_proj(attn.reshape(n_b * seq, d_model), w_o, h, tm, tn)


def _moba_attention(qkv, slopes):
    n_b, seq, _ = qkv.shape
    blk = MOBA_BLOCK
    assert blk == ATTN_BLOCK
    n_blk = seq // blk
    assert n_blk <= SUBLANES
    k_lanes = [SUBLANES, SUBLANES + 1, SUBLANES + 2]
    q_lanes = [SUBLANES + 3, SUBLANES + 4, SUBLANES + 5]

    def body(slopes_ref, q_ref, k_ref, v_ref, o_ref, kaug_ref, qaug_ref, kmean_ref):
        h = pl.program_id(1)
        slope2 = slopes_ref[h] * LOG2E

        kmean_ref[...] = jnp.zeros_like(kmean_ref)
        for n in range(n_blk):
            rows = slice(n * blk, (n + 1) * blk)
            kb = k_ref[0, rows, :]
            kmean_ref[n:n + 1, :] = jnp.mean(kb.astype(F32), axis=0, keepdims=True)
            lane = lax.broadcasted_iota(jnp.int32, (blk, LANES), 1)
            pos = (lax.broadcasted_iota(jnp.int32, (blk, 1), 0) + n * blk).astype(F32)
            hi, mid, lo = _bf16_pieces(slope2 * pos)
            entries = {n: 1.0, k_lanes[0]: hi, k_lanes[1]: mid, k_lanes[2]: lo,
                       q_lanes[0]: 1.0, q_lanes[1]: 1.0, q_lanes[2]: 1.0}
            kaug_ref[rows, :LANES] = kb
            kaug_ref[rows, LANES:] = _lane_table(lane, entries).astype(BF16)

        gate = _qk(kmean_ref[...].astype(BF16), q_ref[0])[:SUBLANES]
        key_blk = lax.broadcasted_iota(jnp.int32, (SUBLANES, seq), 0)
        q_pos = lax.broadcasted_iota(jnp.int32, (SUBLANES, seq), 1)
        q_blk = q_pos // blk
        beaten = jnp.zeros((SUBLANES, seq), F32)
        for n2 in range(n_blk):
            rival = gate[n2:n2 + 1, :]
            wins = jnp.where(key_blk > n2, jnp.where(rival >= gate, 1.0, 0.0),
                             jnp.where(rival > gate, 1.0, 0.0))
            beaten = beaten + jnp.where(n2 < q_blk, wins, 0.0)
        bias = jnp.where(key_blk < q_blk,
                         jnp.where(beaten < MOBA_TOP_K, 0.0, MASK_VALUE),
                         jnp.where(key_blk == q_blk, 0.0, MASK_VALUE))
        hi, mid, lo = _bf16_pieces(-slope2 * q_pos.astype(F32))
        pieces = jnp.where(key_blk < 3, 1.0,
                           jnp.where(key_blk == 3, hi,
                                     jnp.where(key_blk == 4, mid,
                                               jnp.where(key_blk == 5, lo, 0.0))))
        aug_t = jnp.concatenate(
            [bias, pieces, jnp.zeros((LANES - 2 * SUBLANES, seq), F32)], axis=0)
        for qi in range(n_blk):
            rows = slice(qi * blk, (qi + 1) * blk)
            qaug_ref[rows, :] = aug_t[:, rows].T.astype(BF16)
        for qi in range(n_blk):
            rows = slice(qi * blk, (qi + 1) * blk)
            q_full = jnp.concatenate([q_ref[0, rows, :], qaug_ref[rows, :]], axis=1)
            o_ref[0, rows, :] = _causal_softmax_pv(q_full, kaug_ref, v_ref,
                                                   qi + 1).astype(BF16)

    return pl.pallas_call(
        body,
        grid=(n_b, N_HEADS),
        in_specs=[
            pl.BlockSpec(memory_space=pltpu.SMEM),
            pl.BlockSpec((1, seq, HEAD_DIM), lambda b, h: (b, 0, h)),
            pl.BlockSpec((1, seq, HEAD_DIM), lambda b, h: (b, 0, N_HEADS + h)),
            pl.BlockSpec((1, seq, HEAD_DIM), lambda b, h: (b, 0, 2 * N_HEADS + h)),
        ],
        out_specs=pl.BlockSpec((1, seq, HEAD_DIM), lambda b, h: (b, 0, h)),
        out_shape=jax.ShapeDtypeStruct((n_b, seq, N_HEADS * HEAD_DIM), BF16),
        scratch_shapes=[pltpu.VMEM((seq, 2 * LANES), BF16),
                        pltpu.VMEM((seq, LANES), BF16),
                        pltpu.VMEM((2 * SUBLANES, HEAD_DIM), F32)],
        compiler_params=_compiler_params(("parallel", "arbitrary")),
    )(slopes, qkv, qkv, qkv)


def _moba_layer(h, gain, w_qkv, w_o, n_b, seq, tm, tn):
    d_model = h.shape[1]
    qkv = _norm_proj(h, gain, w_qkv, BF16, tm, tn, 3 * d_model, q_blocks=d_model // tn)
    slopes = 2.0 ** (-8.0 * jnp.arange(1, N_HEADS + 1, dtype=F32) / N_HEADS)
    attn = _moba_attention(qkv.reshape(n_b, seq, 3 * d_model), slopes)
    return _out_proj(attn.reshape(n_b * seq, d_model), w_o, h, tm, tn)


S5_BLOCK_GROUPS = LANES // S5_GROUP
S5_PAIR_ROWS = 2 * S5_GROUP
S5_PAIRS = S5_BLOCK_GROUPS // 2


def _s5_pair_tiles(w, n_lb):
    n_g, t_len, n_c, n_p = w.shape
    w = w.reshape(n_lb, S5_PAIRS, 2, t_len, n_c, n_p).transpose(0, 3, 1, 2, 4, 5)
    eye = jnp.eye(2, dtype=w.dtype)
    tiles = w[:, :, :, :, :, None, :] * eye[None, None, None, :, None, :, None]
    return tiles.reshape(n_lb, t_len, S5_PAIRS, 2 * n_c, 2 * n_p)


def _s5_weights(a_re, a_im, log_dt, b_re, b_im, c_re, c_im, n_levels):
    n_g, n_p = a_re.shape
    n_lb = n_g // S5_BLOCK_GROUPS
    t_len = S5_CHUNK
    a = lax.complex(a_re.astype(F32), a_im.astype(F32))
    dt = jnp.exp(log_dt.astype(F32))[:, None]
    a_dt = a * dt
    a_bar = jnp.exp(a_dt)
    b_bar = ((a_bar - 1.0) / a)[..., None] * lax.complex(b_re.astype(F32), b_im.astype(F32))
    c_mat = lax.complex(c_re.astype(F32), c_im.astype(F32))
    steps = jnp.arange(t_len + 1, dtype=F32)
    a_pow = jnp.exp(a_dt[:, None, :] * steps[None, :, None])

    taps = jnp.einsum('gop,gtp,gpi->gtio', c_mat, a_pow[:, :t_len], b_bar,
                      precision=lax.Precision.HIGHEST).real
    taps = taps.reshape(n_lb, S5_BLOCK_GROUPS, t_len, S5_GROUP, S5_GROUP)
    taps = taps.transpose(0, 2, 1, 3, 4)
    eye = jnp.eye(S5_BLOCK_GROUPS, dtype=F32)
    taps = taps[:, :, :, :, None, :] * eye[None, None, :, None, :, None]
    taps = taps.reshape(n_lb, t_len, LANES, LANES)

    inp_c = a_pow[:, t_len - 1::-1][:, :t_len, None, :] * b_bar.transpose(0, 2, 1)[:, None]
    out_c = c_mat[:, None] * a_pow[:, 1:, None, :]
    w_in = (_s5_pair_tiles(inp_c.real, n_lb), _s5_pair_tiles(inp_c.imag, n_lb))
    w_out = (_s5_pair_tiles(out_c.real, n_lb), _s5_pair_tiles(-out_c.imag, n_lb))

    jumps = (t_len * 2 ** jnp.arange(n_levels)).astype(F32)
    d = jnp.exp(a_dt[:, None, :] * jumps[None, :, None])
    d = d.reshape(n_lb, S5_BLOCK_GROUPS, n_levels, n_p).transpose(0, 2, 1, 3)
    d = d.reshape(n_lb, n_levels, S5_BLOCK_GROUPS * n_p)
    decay = jnp.stack([d.real, d.imag], axis=2)
    return taps, w_in, w_out, decay


def _s5_scan(u, taps, w_in, w_out, decay, d_skip, n_chunks):
    t_len, n_rows, d_model = u.shape
    n_levels = decay.shape[1]
    half = decay.shape[3]
    wide = t_len * LANES

    def body(u_ref, taps_ref, wre_ref, wim_ref, vre_ref, vim_ref, dec_ref, d_ref,
             z_ref, a_scr, conv_scr, win_scr, vout_scr):
        @pl.when(pl.program_id(0) == 0)
        def _():
            conv_scr[...] = jnp.zeros_like(conv_scr)
            win_scr[...] = jnp.zeros_like(win_scr)
            vout_scr[...] = jnp.zeros_like(vout_scr)

        for t in range(t_len):
            lanes_t = slice(t * LANES, (t + 1) * LANES)
            a_scr[:, lanes_t] = u_ref[t].astype(BF16)
            for q in range(S5_PAIRS):
                rows = slice(t * LANES + q * S5_PAIR_ROWS, t * LANES + (q + 1) * S5_PAIR_ROWS)
                re_lanes = slice(q * LANES, (q + 1) * LANES)
                im_lanes = slice(half + q * LANES, half + (q + 1) * LANES)
                win_scr[rows, re_lanes] = wre_ref[0, t, q]
                win_scr[rows, im_lanes] = wim_ref[0, t, q]
                vout_scr[rows, re_lanes] = vre_ref[0, t, q]
                vout_scr[rows, im_lanes] = vim_ref[0, t, q]
            for t2 in range(t, t_len):
                conv_scr[lanes_t, t2 * LANES:(t2 + 1) * LANES] = taps_ref[0, t2 - t]

        a = a_scr[...]
        x = jnp.dot(a, win_scr[...], preferred_element_type=F32)
        h_re, h_im = x[:, :half], x[:, half:]
        chunk = lax.broadcasted_iota(jnp.int32, (n_rows, half), 0) % n_chunks

        def shifted(v, n):
            return jnp.where(chunk >= n, pltpu.roll(v, n, 0), 0.0)

        for lvl in range(n_levels):
            p_re, p_im = shifted(h_re, 2 ** lvl), shifted(h_im, 2 ** lvl)
            d_re, d_im = dec_ref[0, lvl, 0:1, :], dec_ref[0, lvl, 1:2, :]
            h_re, h_im = (h_re + d_re * p_re - d_im * p_im,
                          h_im + d_re * p_im + d_im * p_re)
        h_in = jnp.concatenate([shifted(h_re, 1), shifted(h_im, 1)], axis=1).astype(BF16)
        y = jnp.dot(a, conv_scr[...], preferred_element_type=F32)
        y = y + _qk(h_in, vout_scr[...])
        for t in range(t_len):
            z = y[:, t * LANES:(t + 1) * LANES] + d_ref[...] * u_ref[t]
            z_ref[t] = jax.nn.gelu(z).astype(BF16)

    pair_tile = (1, t_len, S5_PAIRS, S5_PAIR_ROWS, LANES)
    pair_spec = pl.BlockSpec(pair_tile, lambda i: (i, 0, 0, 0, 0))
    return pl.pallas_call(
        body,
        grid=(d_model // LANES,),
        in_specs=[
            pl.BlockSpec((t_len, n_rows, LANES), lambda i: (0, 0, i)),
            pl.BlockSpec((1, t_len, LANES, LANES), lambda i: (i, 0, 0, 0)),
            pair_spec, pair_spec, pair_spec, pair_spec,
            pl.BlockSpec((1, n_levels, 2, half), lambda i: (i, 0, 0, 0)),
            pl.BlockSpec((1, LANES), lambda i: (0, i)),
        ],
        out_specs=pl.BlockSpec((t_len, n_rows, LANES), lambda i: (0, 0, i)),
        out_shape=jax.ShapeDtypeStruct((t_len, n_rows, d_model), BF16),
        scratch_shapes=[pltpu.VMEM((n_rows, wide), BF16),
                        pltpu.VMEM((wide, wide), BF16),
                        pltpu.VMEM((wide, 2 * half), BF16),
                        pltpu.VMEM((wide, 2 * half), BF16)],
        compiler_params=_compiler_params(("arbitrary",)),
    )(u, taps, *w_in, *w_out, decay, d_skip)


def _s5_layer(h, gain, w_in, a_re, a_im, log_dt, b_re, b_im, c_re, c_im, d_skip,
              w_glu, n_b, seq, tm, tn):
    m_rows, d_model = h.shape
    t_len = S5_CHUNK
    n_chunks = seq // t_len
    n_levels = max(1, math.ceil(math.log2(n_chunks)))
    hp = h.reshape(n_b, n_chunks, t_len, d_model).transpose(2, 0, 1, 3).reshape(m_rows, d_model)
    u = _norm_proj(hp, gain, w_in, F32, tm, tn, d_model)
    taps, s5_in, s5_out, decay = _s5_weights(a_re, a_im, log_dt, b_re, b_im, c_re, c_im,
                                             n_levels)
    to_bf16 = lambda ws: tuple(w.astype(BF16) for w in ws)
    z = _s5_scan(u.reshape(t_len, n_b * n_chunks, d_model), taps.astype(BF16),
                 to_bf16(s5_in), to_bf16(s5_out), decay,
                 d_skip.astype(F32).reshape(1, d_model), n_chunks)

    def epilogue(accs, ex):
        return ex[0] + accs[0] * _sigmoid(accs[1])

    hp = _fused_matmul([(z.reshape(m_rows, d_model), ())], [], None, [],
                       [(0, *w_glu, 0), (0, *w_glu, d_model // tn)],
                       [(hp, 'tile')], epilogue, F32, d_model, tm, tn)
    return hp.reshape(t_len, n_b, n_chunks, d_model).transpose(1, 2, 0, 3).reshape(m_rows, d_model)


@jax.jit
def _trunk(x, p, norm_g, final_g, w_ffn_in, w_ffn_out, w_ple_gate, w_ple_proj,
           fox_w_in, fox_b_f, fox_w_o, moba_w_qkv, moba_w_o,
           s5_w_in, s5_a_re, s5_a_im, s5_log_dt, s5_b_re, s5_b_im,
           s5_c_re, s5_c_im, s5_d, s5_w_glu):
    n_b, seq, d_model = x.shape
    depth = norm_g.shape[0]
    m_rows = n_b * seq
    tm = min(1024, m_rows)
    tn = 512
    tn_ffn_out = 256
    h = x.reshape(m_rows, d_model)
    p_rows = p.reshape(depth, m_rows, p.shape[-1])
    for i in range(depth):
        g = norm_g[i]
        h = _ffn(h, g[0], (w_ffn_in, (i, 0)), (w_ffn_out, (i, 0)), tn, tn_ffn_out, tm)
        kind, j = i % 3, i // 3
        if kind == 0:
            h = _fox_layer(h, g[1], (fox_w_in, (j,)), fox_b_f[j], (fox_w_o, (j,)),
                           n_b, seq, tm, tn)
        elif kind == 1:
            h = _moba_layer(h, g[1], (moba_w_qkv, (j,)), (moba_w_o, (j,)), n_b, seq, tm, tn)
        else:
            h = _s5_layer(h, g[1], (s5_w_in, (j,)), s5_a_re[j], s5_a_im[j],
                          s5_log_dt[j], s5_b_re[j], s5_b_im[j], s5_c_re[j], s5_c_im[j],
                          s5_d[j], (s5_w_glu, (j,)), n_b, seq, tm, tn)
        h = _ffn(h, g[2], (w_ffn_in, (i, 1)), (w_ffn_out, (i, 1)), tn, tn_ffn_out, tm)
        h = _ple(h, g[3], (p_rows, (i,)), (w_ple_gate, (i,)), (w_ple_proj, (i,)), tm, tn)
    return _final_norm(h, final_g, min(tm, 512)).reshape(n_b, seq, d_model)


def kernel(x, p, norm_g, final_g, w_ffn_in, w_ffn_out, w_ple_gate, w_ple_proj, fox_w_in, fox_b_f, fox_w_o, moba_w_qkv, moba_w_o, s5_w_in, s5_a_re, s5_a_im, s5_log_dt, s5_b_re, s5_b_im, s5_c_re, s5_c_im, s5_d, s5_w_glu):
    return _trunk(x, p, norm_g, final_g, w_ffn_in, w_ffn_out, w_ple_gate, w_ple_proj,
                  fox_w_in, fox_b_f, fox_w_o, moba_w_qkv, moba_w_o,
                  s5_w_in, s5_a_re, s5_a_im, s5_log_dt, s5_b_re, s5_b_im,
                  s5_c_re, s5_c_im, s5_d, s5_w_glu)
```

```python
import functools
import math

import jax
import jax.numpy as jnp
from jax import lax
from jax.experimental import pallas as pl
from jax.experimental.pallas import tpu as pltpu

F32 = jnp.float32
BF16 = jnp.bfloat16

N_HEADS = 16
HEAD_DIM = 128
LANES = 128
SUBLANES = 8
ATTN_BLOCK = 256
MOBA_BLOCK = 256
MOBA_TOP_K = 3
S5_GROUP = 16
S5_STATE = 64
S5_CHUNK = 16
RMS_EPS = 1e-6
MASK_VALUE = -1e30
LOG2E = math.log2(math.e)
Q_PRESCALE = HEAD_DIM ** -0.5 * LOG2E

V7X_VMEM_BYTES = 64 * 1024 * 1024
VMEM_LIMIT_BYTES = V7X_VMEM_BYTES - 8 * 1024 * 1024
PROLOGUE_ROWS = 256
TN_QKV = 1024


def _compiler_params(semantics):
    return pltpu.CompilerParams(dimension_semantics=semantics,
                                vmem_limit_bytes=VMEM_LIMIT_BYTES)


def _sigmoid(x):
    return 1.0 / (1.0 + jnp.exp(-x))


def _rms_rows(x, g):
    ms = jnp.mean(x * x, axis=-1, keepdims=True)
    return x * lax.rsqrt(ms + RMS_EPS) * g


def _prefixed_spec(prefix, block, index_map):
    return pl.BlockSpec((None,) * len(prefix) + block,
                        lambda *ids: prefix + index_map(*ids))


def _fused_matmul(a_list, row_consts, prologue, lhs_widths, terms, extras,
                  epilogue, out_dtype, n_out, tm, tn, valid_cols=None):
    m_rows = a_list[0][0].shape[-2]
    grid = (m_rows // tm, n_out // tn)
    n_a, n_c, n_t, n_e = len(a_list), len(row_consts), len(terms), len(extras)

    def body(*refs):
        a_refs = refs[:n_a]
        c_refs = refs[n_a:n_a + n_c]
        w_refs = refs[n_a + n_c:n_a + n_c + n_t]
        e_refs = refs[n_a + n_c + n_t:n_a + n_c + n_t + n_e]
        o_ref = refs[n_a + n_c + n_t + n_e]
        lhs_refs = refs[n_a + n_c + n_t + n_e + 1:]
        if prologue is not None:
            @pl.when(pl.program_id(1) == 0)
            def _():
                consts = [c[...] for c in c_refs]

                def rows_step(r, carry):
                    rows = pl.ds(pl.multiple_of(r * PROLOGUE_ROWS, PROLOGUE_ROWS),
                                 PROLOGUE_ROWS)
                    outs = prologue([a[rows, :] for a in a_refs], consts)
                    for dst, val in zip(lhs_refs, outs):
                        dst[rows, :] = val
                    return carry

                lax.fori_loop(0, tm // PROLOGUE_ROWS, rows_step, 0)
        else:
            lhs_refs = a_refs
        def weight_tile(w_ref):
            w = w_ref[...]
            if valid_cols is not None:
                lane = lax.broadcasted_iota(jnp.int32, w.shape, 1)
                w = jnp.where(lane < valid_cols, w, 0.0)
            return w.astype(BF16)

        accs = [jnp.dot(lhs_refs[li][...], weight_tile(w), preferred_element_type=F32)
                for (li, _, _, _), w in zip(terms, w_refs)]
        o_ref[...] = epilogue(accs, [e[...] for e in e_refs]).astype(out_dtype)

    in_specs = [_prefixed_spec(pre, (tm, a.shape[-1]), lambda i, j: (i, 0))
                for a, pre in a_list]
    in_specs += [pl.BlockSpec((1, c.shape[1]), lambda i, j: (0, 0)) for c in row_consts]
    for _, w, pre, off in terms:
        in_specs.append(_prefixed_spec(
            pre, (w.shape[-2], tn), functools.partial(lambda i, j, o: (0, j + o), o=off)))
    for arr, kind in extras:
        if kind == 'tile':
            in_specs.append(pl.BlockSpec((tm, tn), lambda i, j: (i, j)))
        else:
            in_specs.append(pl.BlockSpec((1, tn), lambda i, j: (0, j)))
    scratch = []
    if prologue is not None:
        scratch = [pltpu.VMEM((tm, k), BF16) for k in lhs_widths]
    return pl.pallas_call(
        body,
        grid=grid,
        in_specs=in_specs,
        out_specs=pl.BlockSpec((tm, tn), lambda i, j: (i, j)),
        out_shape=jax.ShapeDtypeStruct((m_rows, n_out), out_dtype),
        scratch_shapes=scratch,
        compiler_params=_compiler_params(("parallel", "arbitrary")),
    )(*[a for a, _ in a_list], *row_consts, *[w for _, w, _, _ in terms],
      *[e for e, _ in extras])


def _norm_prologue(a_tiles, consts):
    return [_rms_rows(a_tiles[0], consts[0]).astype(BF16)]


def _ffn(h, gain, w_in, w_out, tn_in, tn_out, tm):
    d_model = h.shape[1]
    d_ff = w_out[0].shape[-2]

    def act_epilogue(accs, _):
        gate, up = accs
        return gate * _sigmoid(gate) * up

    act = _fused_matmul([(h, ())], [gain.reshape(1, d_model)], _norm_prologue, [d_model],
                        [(0, *w_in, 0), (0, *w_in, d_ff // tn_in)], [],
                        act_epilogue, BF16, d_ff, tm, tn_in)

    def res_epilogue(accs, ex):
        return ex[0] + 0.5 * accs[0]

    return _fused_matmul([(act, ())], [], None, [], [(0, *w_out, 0)], [(h, 'tile')],
                         res_epilogue, F32, d_model, tm, tn_out)


def _ple(h, gain, p_i, w_gate, w_proj, tm, tn):
    d_model = h.shape[1]

    def prologue(a_tiles, consts):
        return [_rms_rows(a_tiles[0], consts[0]).astype(BF16), a_tiles[1].astype(BF16)]

    def epilogue(accs, ex):
        return ex[0] + _sigmoid(accs[0]) * accs[1]

    return _fused_matmul([(h, ()), p_i], [gain.reshape(1, d_model)], prologue,
                         [d_model, p_i[0].shape[-1]],
                         [(0, *w_gate, 0), (1, *w_proj, 0)], [(h, 'tile')],
                         epilogue, F32, d_model, tm, tn)


def _norm_proj(h, gain, w, out_dtype, tm, tn, n_out, q_blocks=0):
    d_model = h.shape[1]

    def epilogue(accs, _):
        if not q_blocks:
            return accs[0]
        return accs[0] * jnp.where(pl.program_id(1) < q_blocks, Q_PRESCALE, 1.0)

    return _fused_matmul([(h, ())], [gain.reshape(1, d_model)], _norm_prologue, [d_model],
                         [(0, *w, 0)], [], epilogue, out_dtype, n_out, tm, tn)


def _out_proj(a, w, h, tm, tn):
    return _fused_matmul([(a, ())], [], None, [], [(0, *w, 0)], [(h, 'tile')],
                         lambda accs, ex: ex[0] + accs[0], F32, h.shape[1], tm, tn)


def _final_norm(h, gain, tm):
    m_rows, d_model = h.shape

    def body(h_ref, g_ref, o_ref):
        o_ref[...] = _rms_rows(h_ref[...], g_ref[...])

    return pl.pallas_call(
        body,
        grid=(m_rows // tm,),
        in_specs=[pl.BlockSpec((tm, d_model), lambda i: (i, 0)),
                  pl.BlockSpec((1, d_model), lambda i: (0, 0))],
        out_specs=pl.BlockSpec((tm, d_model), lambda i: (i, 0)),
        out_shape=jax.ShapeDtypeStruct((m_rows, d_model), F32),
        compiler_params=_compiler_params(("parallel",)),
    )(h, gain.reshape(1, d_model))


def _qk(q, k):
    return lax.dot_general(q, k, (((1,), (1,)), ((), ())), preferred_element_type=F32)


def _bf16_pieces(x):
    hi = x.astype(BF16).astype(F32)
    rest = x - hi
    mid = rest.astype(BF16).astype(F32)
    return hi, mid, rest - mid


def _lane_table(lane, entries, default=0.0):
    out = default
    for idx, val in reversed(list(entries.items())):
        out = jnp.where(lane == idx, val, out)
    return out


def _attend_all_tiles(q_ref, qaug_ref, kaug_ref, v_ref, o_ref, n_blk):
    blk = ATTN_BLOCK

    def logits(qi):
        rows = slice(qi * blk, (qi + 1) * blk)
        q_full = jnp.concatenate([q_ref[0, rows, :], qaug_ref[rows, :]], axis=1)
        return _qk(q_full, kaug_ref[0:(qi + 1) * blk, :])

    s = logits(0)
    for qi in range(n_blk):
        s_next = logits(qi + 1) if qi + 1 < n_blk else None
        o_ref[0, qi * blk:(qi + 1) * blk, :] = _softmax_pv(s, v_ref, qi + 1).astype(BF16)
        s = s_next


def _softmax_pv(s, v_ref, n_blocks):
    blk = ATTN_BLOCK
    n_keys = n_blocks * blk
    row = lax.broadcasted_iota(jnp.int32, (blk, blk), 0)
    col = lax.broadcasted_iota(jnp.int32, (blk, blk), 1)
    diag = jnp.where(row >= col, s[:, n_keys - blk:], MASK_VALUE)
    parts = [s[:, i * LANES:(i + 1) * LANES] for i in range((n_keys - blk) // LANES)]
    parts += [diag[:, :LANES], diag[:, LANES:]]
    m = jnp.max(functools.reduce(jnp.maximum, parts), axis=1, keepdims=True)
    probs = [jnp.exp2(part - m) for part in parts]
    denom = jnp.sum(functools.reduce(jnp.add, probs), axis=1, keepdims=True)
    p = jnp.concatenate([pr.astype(BF16) for pr in probs], axis=1)
    acc = jnp.dot(p, v_ref[0, 0:n_keys, :], preferred_element_type=F32)
    return acc / denom


def _log_forget_cumsum(logf):
    n_b, seq, lanes = logf.shape

    def body(x_ref, o_ref):
        x = x_ref[0]
        row = lax.broadcasted_iota(jnp.int32, (seq, lanes), 0)
        shift = 1
        while shift < seq:
            x = x + jnp.where(row >= shift, pltpu.roll(x, shift, 0), 0.0)
            shift *= 2
        o_ref[0] = x

    return pl.pallas_call(
        body,
        grid=(n_b,),
        in_specs=[pl.BlockSpec((1, seq, lanes), lambda b: (b, 0, 0))],
        out_specs=pl.BlockSpec((1, seq, lanes), lambda b: (b, 0, 0)),
        out_shape=jax.ShapeDtypeStruct((n_b, seq, lanes), F32),
        compiler_params=_compiler_params(("parallel",)),
    )(logf)


def _fox_attention(qkv, c):
    n_b, seq, _ = qkv.shape
    blk = ATTN_BLOCK
    n_blk = seq // blk

    def body(q_ref, k_ref, v_ref, c_ref, o_ref, kaug_ref, qaug_ref):
        h = pl.program_id(1)

        for r in range(n_blk):
            rows = slice(r * blk, (r + 1) * blk)
            lane = lax.broadcasted_iota(jnp.int32, (blk, LANES), 1)
            c_head = jnp.sum(jnp.where(lane == h, c_ref[0, rows, :], 0.0),
                             axis=1, keepdims=True) * LOG2E
            hi, mid, lo = _bf16_pieces(c_head)
            q_aug = _lane_table(lane, {0: hi, 1: mid, 2: lo, 3: 1.0, 4: 1.0, 5: 1.0})
            k_aug = _lane_table(lane, {0: 1.0, 1: 1.0, 2: 1.0, 3: -hi, 4: -mid, 5: -lo})
            qaug_ref[rows, :] = q_aug.astype(BF16)
            kaug_ref[rows, :LANES] = k_ref[0, rows, :]
            kaug_ref[rows, LANES:] = k_aug.astype(BF16)
        _attend_all_tiles(q_ref, qaug_ref, kaug_ref, v_ref, o_ref, n_blk)

    return pl.pallas_call(
        body,
        grid=(n_b, N_HEADS),
        in_specs=[
            pl.BlockSpec((1, seq, HEAD_DIM), lambda b, h: (b, 0, h)),
            pl.BlockSpec((1, seq, HEAD_DIM), lambda b, h: (b, 0, N_HEADS + h)),
            pl.BlockSpec((1, seq, HEAD_DIM), lambda b, h: (b, 0, 2 * N_HEADS + h)),
            pl.BlockSpec((1, seq, LANES), lambda b, h: (b, 0, 0)),
        ],
        out_specs=pl.BlockSpec((1, seq, HEAD_DIM), lambda b, h: (b, 0, h)),
        out_shape=jax.ShapeDtypeStruct((n_b, seq, N_HEADS * HEAD_DIM), BF16),
        scratch_shapes=[pltpu.VMEM((seq, 2 * LANES), BF16),
                        pltpu.VMEM((seq, LANES), BF16)],
        compiler_params=_compiler_params(("parallel", "arbitrary")),
    )(qkv, qkv, qkv, c)


def _fox_layer(h, gain, w_in, b_f, w_o, n_b, seq, tm, tn):
    d_model = h.shape[1]
    qkv = _norm_proj(h, gain, w_in, BF16, tm, TN_QKV, 3 * d_model,
                     q_blocks=d_model // TN_QKV)
    b_row = jnp.pad(b_f, (0, LANES - N_HEADS)).reshape(1, LANES)

    def logsig_epilogue(accs, ex):
        z = accs[0] + ex[0]
        return jnp.minimum(z, 0.0) - jnp.log(1.0 + jnp.exp(-jnp.abs(z)))

    logf = _fused_matmul([(h, ())], [gain.reshape(1, d_model)], _norm_prologue, [d_model],
                         [(0, *w_in, 3 * d_model // LANES)], [(b_row, 'row')],
                         logsig_epilogue, F32, LANES, tm, LANES, valid_cols=N_HEADS)
    c = _log_forget_cumsum(logf.reshape(n_b, seq, LANES))
    attn = _fox_attention(qkv.reshape(n_b, seq, 3 * d_model), c)
    return _out_proj(attn.reshape(n_b * seq, d_model), w_o, h, tm, tn)


def _moba_attention(qkv, slopes):
    n_b, seq, _ = qkv.shape
    blk = MOBA_BLOCK
    assert blk == ATTN_BLOCK
    n_blk = seq // blk
    assert n_blk <= SUBLANES
    k_lanes = [SUBLANES, SUBLANES + 1, SUBLANES + 2]
    q_lanes = [SUBLANES + 3, SUBLANES + 4, SUBLANES + 5]

    def body(slopes_ref, q_ref, k_ref, v_ref, o_ref, kaug_ref, qaug_ref, kmean_ref):
        h = pl.program_id(1)
        slope2 = slopes_ref[h] * LOG2E

        kmean_ref[...] = jnp.zeros_like(kmean_ref)
        for n in range(n_blk):
            rows = slice(n * blk, (n + 1) * blk)
            kb = k_ref[0, rows, :]
            kmean_ref[n:n + 1, :] = jnp.mean(kb.astype(F32), axis=0, keepdims=True)
            lane = lax.broadcasted_iota(jnp.int32, (blk, LANES), 1)
            pos = (lax.broadcasted_iota(jnp.int32, (blk, 1), 0) + n * blk).astype(F32)
            hi, mid, lo = _bf16_pieces(slope2 * pos)
            entries = {n: 1.0, k_lanes[0]: hi, k_lanes[1]: mid, k_lanes[2]: lo,
                       q_lanes[0]: 1.0, q_lanes[1]: 1.0, q_lanes[2]: 1.0}
            kaug_ref[rows, :LANES] = kb
            kaug_ref[rows, LANES:] = _lane_table(lane, entries).astype(BF16)

        gate = _qk(kmean_ref[...].astype(BF16), q_ref[0])[:SUBLANES]
        key_blk = lax.broadcasted_iota(jnp.int32, (SUBLANES, seq), 0)
        q_pos = lax.broadcasted_iota(jnp.int32, (SUBLANES, seq), 1)
        q_blk = q_pos // blk
        beaten = jnp.zeros((SUBLANES, seq), F32)
        for n2 in range(n_blk):
            rival = gate[n2:n2 + 1, :]
            wins = jnp.where(key_blk > n2, jnp.where(rival >= gate, 1.0, 0.0),
                             jnp.where(rival > gate, 1.0, 0.0))
            beaten = beaten + jnp.where(n2 < q_blk, wins, 0.0)
        bias = jnp.where(key_blk < q_blk,
                         jnp.where(beaten < MOBA_TOP_K, 0.0, MASK_VALUE),
                         jnp.where(key_blk == q_blk, 0.0, MASK_VALUE))
        hi, mid, lo = _bf16_pieces(-slope2 * q_pos.astype(F32))
        pieces = jnp.where(key_blk < 3, 1.0,
                           jnp.where(key_blk == 3, hi,
                                     jnp.where(key_blk == 4, mid,
                                               jnp.where(key_blk == 5, lo, 0.0))))
        aug_t = jnp.concatenate(
            [bias, pieces, jnp.zeros((LANES - 2 * SUBLANES, seq), F32)], axis=0)
        for qi in range(n_blk):
            rows = slice(qi * blk, (qi + 1) * blk)
            qaug_ref[rows, :] = aug_t[:, rows].T.astype(BF16)
        _attend_all_tiles(q_ref, qaug_ref, kaug_ref, v_ref, o_ref, n_blk)

    return pl.pallas_call(
        body,
        grid=(n_b, N_HEADS),
        in_specs=[
            pl.BlockSpec(memory_space=pltpu.SMEM),
            pl.BlockSpec((1, seq, HEAD_DIM), lambda b, h: (b, 0, h)),
            pl.BlockSpec((1, seq, HEAD_DIM), lambda b, h: (b, 0, N_HEADS + h)),
            pl.BlockSpec((1, seq, HEAD_DIM), lambda b, h: (b, 0, 2 * N_HEADS + h)),
        ],
        out_specs=pl.BlockSpec((1, seq, HEAD_DIM), lambda b, h: (b, 0, h)),
        out_shape=jax.ShapeDtypeStruct((n_b, seq, N_HEADS * HEAD_DIM), BF16),
        scratch_shapes=[pltpu.VMEM((seq, 2 * LANES), BF16),
                        pltpu.VMEM((seq, LANES), BF16),
                        pltpu.VMEM((2 * SUBLANES, HEAD_DIM), F32)],
        compiler_params=_compiler_params(("parallel", "arbitrary")),
    )(slopes, qkv, qkv, qkv)


def _moba_layer(h, gain, w_qkv, w_o, n_b, seq, tm, tn):
    d_model = h.shape[1]
    qkv = _norm_proj(h, gain, w_qkv, BF16, tm, TN_QKV, 3 * d_model,
                     q_blocks=d_model // TN_QKV)
    slopes = 2.0 ** (-8.0 * jnp.arange(1, N_HEADS + 1, dtype=F32) / N_HEADS)
    attn = _moba_attention(qkv.reshape(n_b, seq, 3 * d_model), slopes)
    return _out_proj(attn.reshape(n_b * seq, d_model), w_o, h, tm, tn)


S5_BLOCK_GROUPS = LANES // S5_GROUP
S5_PAIR_ROWS = 2 * S5_GROUP
S5_PAIRS = S5_BLOCK_GROUPS // 2


def _s5_pair_tiles(w, n_lb):
    n_g, n_k, n_c, n_p = w.shape
    w = w.reshape(n_lb, S5_PAIRS, 2, n_k, n_c, n_p)
    even = jnp.pad(w[:, :, 0], ((0, 0),) * 4 + ((0, n_p),))
    odd = jnp.pad(w[:, :, 1], ((0, 0),) * 4 + ((n_p, 0),))
    return jnp.concatenate([even, odd], axis=3).astype(BF16)


def _s5_weights(a_re, a_im, log_dt, b_re, b_im, c_re, c_im, n_levels):
    n_g, n_p = a_re.shape
    n_lb = n_g // S5_BLOCK_GROUPS
    t_len = S5_CHUNK
    a = lax.complex(a_re.astype(F32), a_im.astype(F32))
    dt = jnp.exp(log_dt.astype(F32))[:, None]
    a_dt = a * dt
    a_bar = jnp.exp(a_dt)
    b_bar = ((a_bar - 1.0) / a)[..., None] * lax.complex(b_re.astype(F32), b_im.astype(F32))
    c_mat = lax.complex(c_re.astype(F32), c_im.astype(F32))
    steps = jnp.arange(t_len + 1, dtype=F32)
    a_pow = jnp.exp(a_dt[:, None, :] * steps[None, :, None])

    inp_c = a_pow[:, t_len - 1::-1][:, :t_len, None, :] * b_bar.transpose(0, 2, 1)[:, None]
    out_c = c_mat[:, None] * a_pow[:, :, None, :]
    w_in = (_s5_pair_tiles(inp_c.real, n_lb), _s5_pair_tiles(inp_c.imag, n_lb))
    w_out = (_s5_pair_tiles(out_c.real, n_lb), _s5_pair_tiles(-out_c.imag, n_lb))

    jumps = (t_len * 2 ** jnp.arange(n_levels)).astype(F32)
    d = jnp.exp(a_dt[:, None, :] * jumps[None, :, None])
    d = d.reshape(n_lb, S5_BLOCK_GROUPS, n_levels, n_p).transpose(0, 2, 1, 3)
    d = d.reshape(n_lb, n_levels, S5_BLOCK_GROUPS * n_p)
    decay = jnp.stack([d.real, d.imag], axis=2)
    return w_in, w_out, decay


def _s5_scan(u, w_in, w_out, decay, d_skip, n_chunks):
    t_len, n_rows, d_model = u.shape
    n_levels = decay.shape[1]
    half = decay.shape[3]
    wide = t_len * LANES

    def body(u_ref, wre_ref, wim_ref, vre_ref, vim_ref, dec_ref, d_ref,
             z_ref, a_scr, conv_scr, win_scr, vpow_scr):
        @pl.when(pl.program_id(0) == 0)
        def _():
            conv_scr[...] = jnp.zeros_like(conv_scr)
            win_scr[...] = jnp.zeros_like(win_scr)
            vpow_scr[...] = jnp.zeros_like(vpow_scr)

        def place(dst, k, re_ref, im_ref):
            for q in range(S5_PAIRS):
                rows = slice(k * LANES + q * S5_PAIR_ROWS, k * LANES + (q + 1) * S5_PAIR_ROWS)
                dst[rows, q * LANES:(q + 1) * LANES] = re_ref[0, q, k]
                dst[rows, half + q * LANES:half + (q + 1) * LANES] = im_ref[0, q, k]

        for t in range(t_len):
            a_scr[:, t * LANES:(t + 1) * LANES] = u_ref[t].astype(BF16)
            place(win_scr, t, wre_ref, wim_ref)
        for k in range(t_len + 1):
            place(vpow_scr, k, vre_ref, vim_ref)
        taps = _qk(win_scr[(t_len - 1) * LANES:, :], vpow_scr[:wide, :]).astype(BF16)
        for t in range(t_len):
            for t2 in range(t, t_len):
                conv_scr[t * LANES:(t + 1) * LANES, t2 * LANES:(t2 + 1) * LANES] = (
                    taps[:, (t2 - t) * LANES:(t2 - t + 1) * LANES])

        a = a_scr[...]
        x = jnp.dot(a, win_scr[...], preferred_element_type=F32)
        h_re, h_im = x[:, :half], x[:, half:]
        chunk = lax.broadcasted_iota(jnp.int32, (n_rows, half), 0) % n_chunks

        def shifted(v, n):
            return jnp.where(chunk >= n, pltpu.roll(v, n, 0), 0.0)

        for lvl in range(n_levels):
            p_re, p_im = shifted(h_re, 2 ** lvl), shifted(h_im, 2 ** lvl)
            d_re, d_im = dec_ref[0, lvl, 0:1, :], dec_ref[0, lvl, 1:2, :]
            h_re, h_im = (h_re + d_re * p_re - d_im * p_im,
                          h_im + d_re * p_im + d_im * p_re)
        h_in = jnp.concatenate([shifted(h_re, 1), shifted(h_im, 1)], axis=1).astype(BF16)
        y = jnp.dot(a, conv_scr[...], preferred_element_type=F32)
        y = y + _qk(h_in, vpow_scr[LANES:, :])
        for t in range(t_len):
            z = y[:, t * LANES:(t + 1) * LANES] + d_ref[...] * u_ref[t]
            z_ref[t] = jax.nn.gelu(z).astype(BF16)

    def pair_spec(n_k):
        return pl.BlockSpec((1, S5_PAIRS, n_k, S5_PAIR_ROWS, LANES),
                            lambda i: (i, 0, 0, 0, 0))

    return pl.pallas_call(
        body,
        grid=(d_model // LANES,),
        in_specs=[
            pl.BlockSpec((t_len, n_rows, LANES), lambda i: (0, 0, i)),
            pair_spec(t_len), pair_spec(t_len), pair_spec(t_len + 1), pair_spec(t_len + 1),
            pl.BlockSpec((1, n_levels, 2, half), lambda i: (i, 0, 0, 0)),
            pl.BlockSpec((1, LANES), lambda i: (0, i)),
        ],
        out_specs=pl.BlockSpec((t_len, n_rows, LANES), lambda i: (0, 0, i)),
        out_shape=jax.ShapeDtypeStruct((t_len, n_rows, d_model), BF16),
        scratch_shapes=[pltpu.VMEM((n_rows, wide), BF16),
                        pltpu.VMEM((wide, wide), BF16),
                        pltpu.VMEM((wide, 2 * half), BF16),
                        pltpu.VMEM((wide + LANES, 2 * half), BF16)],
        compiler_params=_compiler_params(("arbitrary",)),
    )(u, *w_in, *w_out, decay, d_skip)


def _s5_layer(h, gain, w_in, a_re, a_im, log_dt, b_re, b_im, c_re, c_im, d_skip,
              w_glu, n_b, seq, tm, tn):
    m_rows, d_model = h.shape
    t_len = S5_CHUNK
    n_chunks = seq // t_len
    n_levels = max(1, math.ceil(math.log2(n_chunks)))
    hp = h.reshape(n_b, n_chunks, t_len, d_model).transpose(2, 0, 1, 3).reshape(m_rows, d_model)
    u = _norm_proj(hp, gain, w_in, F32, tm, tn, d_model)
    s5_in, s5_out, decay = _s5_weights(a_re, a_im, log_dt, b_re, b_im, c_re, c_im, n_levels)
    z = _s5_scan(u.reshape(t_len, n_b * n_chunks, d_model), s5_in, s5_out, decay,
                 d_skip.astype(F32).reshape(1, d_model), n_chunks)

    def epilogue(accs, ex):
        return ex[0] + accs[0] * _sigmoid(accs[1])

    hp = _fused_matmul([(z.reshape(m_rows, d_model), ())], [], None, [],
                       [(0, *w_glu, 0), (0, *w_glu, d_model // tn)],
                       [(hp, 'tile')], epilogue, F32, d_model, tm, tn)
    return hp.reshape(t_len, n_b, n_chunks, d_model).transpose(1, 2, 0, 3).reshape(m_rows, d_model)


@jax.jit
def _trunk(x, p, norm_g, final_g, w_ffn_in, w_ffn_out, w_ple_gate, w_ple_proj,
           fox_w_in, fox_b_f, fox_w_o, moba_w_qkv, moba_w_o,
           s5_w_in, s5_a_re, s5_a_im, s5_log_dt, s5_b_re, s5_b_im,
           s5_c_re, s5_c_im, s5_d, s5_w_glu):
    n_b, seq, d_model = x.shape
    depth = norm_g.shape[0]
    m_rows = n_b * seq
    tm = min(1024, m_rows)
    tn = 512
    tn_ffn_out = 256
    h = x.reshape(m_rows, d_model)
    p_rows = p.reshape(depth, m_rows, p.shape[-1])
    for i in range(depth):
        g = norm_g[i]
        h = _ffn(h, g[0], (w_ffn_in, (i, 0)), (w_ffn_out, (i, 0)), tn, tn_ffn_out, tm)
        kind, j = i % 3, i // 3
        if kind == 0:
            h = _fox_layer(h, g[1], (fox_w_in, (j,)), fox_b_f[j], (fox_w_o, (j,)),
                           n_b, seq, tm, tn)
        elif kind == 1:
            h = _moba_layer(h, g[1], (moba_w_qkv, (j,)), (moba_w_o, (j,)), n_b, seq, tm, tn)
        else:
            h = _s5_layer(h, g[1], (s5_w_in, (j,)), s5_a_re[j], s5_a_im[j],
                          s5_log_dt[j], s5_b_re[j], s5_b_im[j], s5_c_re[j], s5_c_im[j],
                          s5_d[j], (s5_w_glu, (j,)), n_b, seq, tm, tn)
        h = _ffn(h, g[2], (w_ffn_in, (i, 1)), (w_ffn_out, (i, 1)), tn, tn_ffn_out, tm)
        h = _ple(h, g[3], (p_rows, (i,)), (w_ple_gate, (i,)), (w_ple_proj, (i,)), tm, tn)
    return _final_norm(h, final_g, min(tm, 512)).reshape(n_b, seq, d_model)


def kernel(x, p, norm_g, final_g, w_ffn_in, w_ffn_out, w_ple_gate, w_ple_proj, fox_w_in, fox_b_f, fox_w_o, moba_w_qkv, moba_w_o, s5_w_in, s5_a_re, s5_a_im, s5_log_dt, s5_b_re, s5_b_im, s5_c_re, s5_c_im, s5_d, s5_w_glu):
    return _trunk(x, p, norm_g, final_g, w_ffn_in, w_ffn_out, w_ple_gate, w_ple_proj,
                  fox_w_in, fox_b_f, fox_w_o, moba_w_qkv, moba_w_o,
                  s5_w_in, s5_a_re, s5_a_im, s5_log_dt, s5_b_re, s5_b_im,
                  s5_c_re, s5_c_im, s5_d, s5_w_glu)
```

```python
import functools
import math

import jax
import jax.numpy as jnp
from jax import lax
from jax.experimental import pallas as pl
from jax.experimental.pallas import tpu as pltpu

F32 = jnp.float32
BF16 = jnp.bfloat16

N_HEADS = 16
HEAD_DIM = 128
LANES = 128
SUBLANES = 8
ATTN_BLOCK = 256
MOBA_BLOCK = 256
MOBA_TOP_K = 3
S5_GROUP = 16
S5_STATE = 64
S5_CHUNK = 16
RMS_EPS = 1e-6
MASK_VALUE = -1e30
LOG2E = math.log2(math.e)
Q_PRESCALE = HEAD_DIM ** -0.5 * LOG2E

V7X_VMEM_BYTES = 64 * 1024 * 1024
VMEM_LIMIT_BYTES = V7X_VMEM_BYTES - 8 * 1024 * 1024
PROLOGUE_ROWS = 256
TN_QKV = 1024
TM_RESIDENT = 256


def _compiler_params(semantics):
    return pltpu.CompilerParams(dimension_semantics=semantics,
                                vmem_limit_bytes=VMEM_LIMIT_BYTES)


def _sigmoid(x):
    return 1.0 / (1.0 + jnp.exp(-x))


def _rms_rows(x, g):
    ms = jnp.mean(x * x, axis=-1, keepdims=True)
    return x * lax.rsqrt(ms + RMS_EPS) * g


def _prefixed_spec(prefix, block, index_map):
    return pl.BlockSpec((None,) * len(prefix) + block,
                        lambda *ids: prefix + index_map(*ids))


def _fused_matmul(a_list, row_consts, prologue, lhs_widths, terms, extras,
                  epilogue, out_dtype, n_out, tm, tn, valid_cols=None):
    m_rows = a_list[0][0].shape[-2]
    grid = (m_rows // tm, n_out // tn)
    n_a, n_c, n_t, n_e = len(a_list), len(row_consts), len(terms), len(extras)

    def body(*refs):
        a_refs = refs[:n_a]
        c_refs = refs[n_a:n_a + n_c]
        w_refs = refs[n_a + n_c:n_a + n_c + n_t]
        e_refs = refs[n_a + n_c + n_t:n_a + n_c + n_t + n_e]
        o_ref = refs[n_a + n_c + n_t + n_e]
        lhs_refs = refs[n_a + n_c + n_t + n_e + 1:]
        if prologue is not None:
            @pl.when(pl.program_id(1) == 0)
            def _():
                consts = [c[...] for c in c_refs]

                def rows_step(r, carry):
                    rows = pl.ds(pl.multiple_of(r * PROLOGUE_ROWS, PROLOGUE_ROWS),
                                 PROLOGUE_ROWS)
                    outs = prologue([a[rows, :] for a in a_refs], consts)
                    for dst, val in zip(lhs_refs, outs):
                        dst[rows, :] = val
                    return carry

                lax.fori_loop(0, tm // PROLOGUE_ROWS, rows_step, 0)
        else:
            lhs_refs = a_refs
        def weight_tile(w_ref):
            w = w_ref[...]
            if valid_cols is not None:
                lane = lax.broadcasted_iota(jnp.int32, w.shape, 1)
                w = jnp.where(lane < valid_cols, w, 0.0)
            return w.astype(BF16)

        accs = [jnp.dot(lhs_refs[li][...], weight_tile(w), preferred_element_type=F32)
                for (li, _, _, _), w in zip(terms, w_refs)]
        o_ref[...] = epilogue(accs, [e[...] for e in e_refs]).astype(out_dtype)

    in_specs = [_prefixed_spec(pre, (tm, a.shape[-1]), lambda i, j: (i, 0))
                for a, pre in a_list]
    in_specs += [pl.BlockSpec((1, c.shape[1]), lambda i, j: (0, 0)) for c in row_consts]
    for _, w, pre, off in terms:
        in_specs.append(_prefixed_spec(
            pre, (w.shape[-2], tn), functools.partial(lambda i, j, o: (0, j + o), o=off)))
    for arr, kind in extras:
        if kind == 'tile':
            in_specs.append(pl.BlockSpec((tm, tn), lambda i, j: (i, j)))
        else:
            in_specs.append(pl.BlockSpec((1, tn), lambda i, j: (0, j)))
    scratch = []
    if prologue is not None:
        scratch = [pltpu.VMEM((tm, k), BF16) for k in lhs_widths]
    return pl.pallas_call(
        body,
        grid=grid,
        in_specs=in_specs,
        out_specs=pl.BlockSpec((tm, tn), lambda i, j: (i, j)),
        out_shape=jax.ShapeDtypeStruct((m_rows, n_out), out_dtype),
        scratch_shapes=scratch,
        compiler_params=_compiler_params(("parallel", "arbitrary")),
    )(*[a for a, _ in a_list], *row_consts, *[w for _, w, _, _ in terms],
      *[e for e, _ in extras])


def _norm_prologue(a_tiles, consts):
    return [_rms_rows(a_tiles[0], consts[0]).astype(BF16)]


def _resident_matmul(a_list, row_consts, prologue, terms, epilogue, out_dtype, tm):
    m_rows = a_list[0][0].shape[-2]
    n_out = terms[0][1].shape[-1]
    n_a, n_c, n_t = len(a_list), len(row_consts), len(terms)
    cast_cols = 512

    def body(*refs):
        a_refs = refs[:n_a]
        c_refs = refs[n_a:n_a + n_c]
        w_refs = refs[n_a + n_c:n_a + n_c + n_t]
        o_ref = refs[n_a + n_c + n_t]
        wb_refs = refs[n_a + n_c + n_t + 1:]

        @pl.when(pl.program_id(0) == 0)
        def _():
            for w_ref, wb_ref in zip(w_refs, wb_refs):
                for c0 in range(0, n_out, cast_cols):
                    cols = slice(c0, c0 + cast_cols)
                    wb_ref[:, cols] = w_ref[:, cols].astype(BF16)

        a_tiles = [a[...] for a in a_refs]
        consts = [c[...] for c in c_refs]
        lhs = prologue(a_tiles, consts)
        accs = [jnp.dot(lhs[li], wb[...], preferred_element_type=F32)
                for (li, _, _), wb in zip(terms, wb_refs)]
        o_ref[...] = epilogue(accs, a_tiles, consts).astype(out_dtype)

    in_specs = [_prefixed_spec(pre, (tm, a.shape[-1]), lambda i: (i, 0)) for a, pre in a_list]
    in_specs += [pl.BlockSpec((1, c.shape[1]), lambda i: (0, 0)) for c in row_consts]
    for _, w, pre in terms:
        in_specs.append(pl.BlockSpec((None,) * len(pre) + w.shape[-2:],
                                     functools.partial(lambda i, p: p + (0, 0), p=pre),
                                     pipeline_mode=pl.Buffered(1)))
    return pl.pallas_call(
        body,
        grid=(m_rows // tm,),
        in_specs=in_specs,
        out_specs=pl.BlockSpec((tm, n_out), lambda i: (i, 0)),
        out_shape=jax.ShapeDtypeStruct((m_rows, n_out), out_dtype),
        scratch_shapes=[pltpu.VMEM(w.shape[-2:], BF16) for _, w, _ in terms],
        compiler_params=_compiler_params(("arbitrary",)),
    )(*[a for a, _ in a_list], *row_consts, *[w for _, w, _ in terms])


def _ffn(h, gain, w_in, w_out, tn_in, tn_out, tm):
    d_model = h.shape[1]
    d_ff = w_out[0].shape[-2]

    def act_epilogue(accs, _):
        gate, up = accs
        return gate * _sigmoid(gate) * up

    act = _fused_matmul([(h, ())], [gain.reshape(1, d_model)], _norm_prologue, [d_model],
                        [(0, *w_in, 0), (0, *w_in, d_ff // tn_in)], [],
                        act_epilogue, BF16, d_ff, tm, tn_in)

    def res_epilogue(accs, ex):
        return ex[0] + 0.5 * accs[0]

    return _fused_matmul([(act, ())], [], None, [], [(0, *w_out, 0)], [(h, 'tile')],
                         res_epilogue, F32, d_model, tm, tn_out)


def _ple(h, gain, p_i, w_gate, w_proj, tm, final_gain=None):
    d_model = h.shape[1]
    consts = [gain.reshape(1, d_model)]
    if final_gain is not None:
        consts.append(final_gain.reshape(1, d_model))

    def prologue(a_tiles, consts):
        return [_rms_rows(a_tiles[0], consts[0]).astype(BF16), a_tiles[1].astype(BF16)]

    def epilogue(accs, a_tiles, consts):
        out = a_tiles[0] + _sigmoid(accs[0]) * accs[1]
        return out if final_gain is None else _rms_rows(out, consts[1])

    return _resident_matmul([(h, ()), p_i], consts, prologue,
                            [(0, *w_gate), (1, *w_proj)], epilogue, F32, tm)


def _norm_proj(h, gain, w, out_dtype, tm, tn, n_out, q_blocks=0):
    d_model = h.shape[1]

    def epilogue(accs, _):
        if not q_blocks:
            return accs[0]
        return accs[0] * jnp.where(pl.program_id(1) < q_blocks, Q_PRESCALE, 1.0)

    return _fused_matmul([(h, ())], [gain.reshape(1, d_model)], _norm_prologue, [d_model],
                         [(0, *w, 0)], [], epilogue, out_dtype, n_out, tm, tn)


def _out_proj(a, w, h, tm):
    return _resident_matmul([(a, ()), (h, ())], [], lambda a_tiles, _: [a_tiles[0]],
                            [(0, *w)], lambda accs, a_tiles, _: a_tiles[1] + accs[0],
                            F32, tm)


def _norm_proj_resident(h, gain, w, tm):
    return _resident_matmul([(h, ())], [gain.reshape(1, h.shape[1])], _norm_prologue,
                            [(0, *w)], lambda accs, _a, _c: accs[0], F32, tm)


def _qk(q, k):
    return lax.dot_general(q, k, (((1,), (1,)), ((), ())), preferred_element_type=F32)


def _bf16_pieces(x):
    hi = x.astype(BF16).astype(F32)
    rest = x - hi
    mid = rest.astype(BF16).astype(F32)
    return hi, mid, rest - mid


def _lane_table(lane, entries, default=0.0):
    out = default
    for idx, val in reversed(list(entries.items())):
        out = jnp.where(lane == idx, val, out)
    return out


def _attend_all_tiles(q_ref, qaug_ref, kaug_ref, v_ref, o_ref, n_blk):
    blk = ATTN_BLOCK

    def logits(qi):
        rows = slice(qi * blk, (qi + 1) * blk)
        q_full = jnp.concatenate([q_ref[0, rows, :], qaug_ref[rows, :]], axis=1)
        return _qk(q_full, kaug_ref[0:(qi + 1) * blk, :])

    s = logits(0)
    for qi in range(n_blk):
        s_next = logits(qi + 1) if qi + 1 < n_blk else None
        o_ref[0, qi * blk:(qi + 1) * blk, :] = _softmax_pv(s, v_ref, qi + 1).astype(BF16)
        s = s_next


def _softmax_pv(s, v_ref, n_blocks):
    blk = ATTN_BLOCK
    n_keys = n_blocks * blk
    row = lax.broadcasted_iota(jnp.int32, (blk, blk), 0)
    col = lax.broadcasted_iota(jnp.int32, (blk, blk), 1)
    diag = jnp.where(row >= col, s[:, n_keys - blk:], MASK_VALUE)
    parts = [s[:, i * LANES:(i + 1) * LANES] for i in range((n_keys - blk) // LANES)]
    parts += [diag[:, :LANES], diag[:, LANES:]]
    m = jnp.max(functools.reduce(jnp.maximum, parts), axis=1, keepdims=True)
    probs = [jnp.exp2(part - m) for part in parts]
    denom = jnp.sum(functools.reduce(jnp.add, probs), axis=1, keepdims=True)
    p = jnp.concatenate([pr.astype(BF16) for pr in probs], axis=1)
    acc = jnp.dot(p, v_ref[0, 0:n_keys, :], preferred_element_type=F32)
    return acc / denom


def _log_forget_cumsum(logf):
    n_b, seq, lanes = logf.shape

    def body(x_ref, o_ref):
        x = x_ref[0]
        row = lax.broadcasted_iota(jnp.int32, (seq, lanes), 0)
        shift = 1
        while shift < seq:
            x = x + jnp.where(row >= shift, pltpu.roll(x, shift, 0), 0.0)
            shift *= 2
        o_ref[0] = x

    return pl.pallas_call(
        body,
        grid=(n_b,),
        in_specs=[pl.BlockSpec((1, seq, lanes), lambda b: (b, 0, 0))],
        out_specs=pl.BlockSpec((1, seq, lanes), lambda b: (b, 0, 0)),
        out_shape=jax.ShapeDtypeStruct((n_b, seq, lanes), F32),
        compiler_params=_compiler_params(("parallel",)),
    )(logf)


def _fox_attention(qkv, c):
    n_b, seq, _ = qkv.shape
    blk = ATTN_BLOCK
    n_blk = seq // blk

    def body(q_ref, k_ref, v_ref, c_ref, o_ref, kaug_ref, qaug_ref):
        h = pl.program_id(1)

        for r in range(n_blk):
            rows = slice(r * blk, (r + 1) * blk)
            lane = lax.broadcasted_iota(jnp.int32, (blk, LANES), 1)
            c_head = jnp.sum(jnp.where(lane == h, c_ref[0, rows, :], 0.0),
                             axis=1, keepdims=True) * LOG2E
            hi, mid, lo = _bf16_pieces(c_head)
            q_aug = _lane_table(lane, {0: hi, 1: mid, 2: lo, 3: 1.0, 4: 1.0, 5: 1.0})
            k_aug = _lane_table(lane, {0: 1.0, 1: 1.0, 2: 1.0, 3: -hi, 4: -mid, 5: -lo})
            qaug_ref[rows, :] = q_aug.astype(BF16)
            kaug_ref[rows, :LANES] = k_ref[0, rows, :]
            kaug_ref[rows, LANES:] = k_aug.astype(BF16)
        _attend_all_tiles(q_ref, qaug_ref, kaug_ref, v_ref, o_ref, n_blk)

    return pl.pallas_call(
        body,
        grid=(n_b, N_HEADS),
        in_specs=[
            pl.BlockSpec((1, seq, HEAD_DIM), lambda b, h: (b, 0, h)),
            pl.BlockSpec((1, seq, HEAD_DIM), lambda b, h: (b, 0, N_HEADS + h)),
            pl.BlockSpec((1, seq, HEAD_DIM), lambda b, h: (b, 0, 2 * N_HEADS + h)),
            pl.BlockSpec((1, seq, LANES), lambda b, h: (b, 0, 0)),
        ],
        out_specs=pl.BlockSpec((1, seq, HEAD_DIM), lambda b, h: (b, 0, h)),
        out_shape=jax.ShapeDtypeStruct((n_b, seq, N_HEADS * HEAD_DIM), BF16),
        scratch_shapes=[pltpu.VMEM((seq, 2 * LANES), BF16),
                        pltpu.VMEM((seq, LANES), BF16)],
        compiler_params=_compiler_params(("parallel", "arbitrary")),
    )(qkv, qkv, qkv, c)


def _fox_layer(h, gain, w_in, b_f, w_o, n_b, seq, tm, tn):
    d_model = h.shape[1]
    qkv = _norm_proj(h, gain, w_in, BF16, tm, TN_QKV, 3 * d_model,
                     q_blocks=d_model // TN_QKV)
    b_row = jnp.pad(b_f, (0, LANES - N_HEADS)).reshape(1, LANES)

    def logsig_epilogue(accs, ex):
        z = accs[0] + ex[0]
        return jnp.minimum(z, 0.0) - jnp.log(1.0 + jnp.exp(-jnp.abs(z)))

    logf = _fused_matmul([(h, ())], [gain.reshape(1, d_model)], _norm_prologue, [d_model],
                         [(0, *w_in, 3 * d_model // LANES)], [(b_row, 'row')],
                         logsig_epilogue, F32, LANES, tm, LANES, valid_cols=N_HEADS)
    c = _log_forget_cumsum(logf.reshape(n_b, seq, LANES))
    attn = _fox_attention(qkv.reshape(n_b, seq, 3 * d_model), c)
    return _out_proj(attn.reshape(n_b * seq, d_model), w_o, h, TM_RESIDENT)


def _moba_attention(qkv, slopes):
    n_b, seq, _ = qkv.shape
    blk = MOBA_BLOCK
    assert blk == ATTN_BLOCK
    n_blk = seq // blk
    assert n_blk <= SUBLANES
    k_lanes = [SUBLANES, SUBLANES + 1, SUBLANES + 2]
    q_lanes = [SUBLANES + 3, SUBLANES + 4, SUBLANES + 5]

    def body(slopes_ref, q_ref, k_ref, v_ref, o_ref, kaug_ref, qaug_ref, kmean_ref):
        h = pl.program_id(1)
        slope2 = slopes_ref[h] * LOG2E

        kmean_ref[...] = jnp.zeros_like(kmean_ref)
        for n in range(n_blk):
            rows = slice(n * blk, (n + 1) * blk)
            kb = k_ref[0, rows, :]
            kmean_ref[n:n + 1, :] = jnp.mean(kb.astype(F32), axis=0, keepdims=True)
            lane = lax.broadcasted_iota(jnp.int32, (blk, LANES), 1)
            pos = (lax.broadcasted_iota(jnp.int32, (blk, 1), 0) + n * blk).astype(F32)
            hi, mid, lo = _bf16_pieces(slope2 * pos)
            entries = {n: 1.0, k_lanes[0]: hi, k_lanes[1]: mid, k_lanes[2]: lo,
                       q_lanes[0]: 1.0, q_lanes[1]: 1.0, q_lanes[2]: 1.0}
            kaug_ref[rows, :LANES] = kb
            kaug_ref[rows, LANES:] = _lane_table(lane, entries).astype(BF16)

        gate = _qk(kmean_ref[...].astype(BF16), q_ref[0])[:SUBLANES]
        key_blk = lax.broadcasted_iota(jnp.int32, (SUBLANES, seq), 0)
        q_pos = lax.broadcasted_iota(jnp.int32, (SUBLANES, seq), 1)
        q_blk = q_pos // blk
        beaten = jnp.zeros((SUBLANES, seq), F32)
        for n2 in range(n_blk):
            rival = gate[n2:n2 + 1, :]
            wins = jnp.where(key_blk > n2, jnp.where(rival >= gate, 1.0, 0.0),
                             jnp.where(rival > gate, 1.0, 0.0))
            beaten = beaten + jnp.where(n2 < q_blk, wins, 0.0)
        bias = jnp.where(key_blk < q_blk,
                         jnp.where(beaten < MOBA_TOP_K, 0.0, MASK_VALUE),
                         jnp.where(key_blk == q_blk, 0.0, MASK_VALUE))
        hi, mid, lo = _bf16_pieces(-slope2 * q_pos.astype(F32))
        pieces = jnp.where(key_blk < 3, 1.0,
                           jnp.where(key_blk == 3, hi,
                                     jnp.where(key_blk == 4, mid,
                                               jnp.where(key_blk == 5, lo, 0.0))))
        aug_t = jnp.concatenate(
            [bias, pieces, jnp.zeros((LANES - 2 * SUBLANES, seq), F32)], axis=0)
        for qi in range(n_blk):
            rows = slice(qi * blk, (qi + 1) * blk)
            qaug_ref[rows, :] = aug_t[:, rows].T.astype(BF16)
        _attend_all_tiles(q_ref, qaug_ref, kaug_ref, v_ref, o_ref, n_blk)

    return pl.pallas_call(
        body,
        grid=(n_b, N_HEADS),
        in_specs=[
            pl.BlockSpec(memory_space=pltpu.SMEM),
            pl.BlockSpec((1, seq, HEAD_DIM), lambda b, h: (b, 0, h)),
            pl.BlockSpec((1, seq, HEAD_DIM), lambda b, h: (b, 0, N_HEADS + h)),
            pl.BlockSpec((1, seq, HEAD_DIM), lambda b, h: (b, 0, 2 * N_HEADS + h)),
        ],
        out_specs=pl.BlockSpec((1, seq, HEAD_DIM), lambda b, h: (b, 0, h)),
        out_shape=jax.ShapeDtypeStruct((n_b, seq, N_HEADS * HEAD_DIM), BF16),
        scratch_shapes=[pltpu.VMEM((seq, 2 * LANES), BF16),
                        pltpu.VMEM((seq, LANES), BF16),
                        pltpu.VMEM((2 * SUBLANES, HEAD_DIM), F32)],
        compiler_params=_compiler_params(("parallel", "arbitrary")),
    )(slopes, qkv, qkv, qkv)


def _moba_layer(h, gain, w_qkv, w_o, n_b, seq, tm, tn):
    d_model = h.shape[1]
    qkv = _norm_proj(h, gain, w_qkv, BF16, tm, TN_QKV, 3 * d_model,
                     q_blocks=d_model // TN_QKV)
    slopes = 2.0 ** (-8.0 * jnp.arange(1, N_HEADS + 1, dtype=F32) / N_HEADS)
    attn = _moba_attention(qkv.reshape(n_b, seq, 3 * d_model), slopes)
    return _out_proj(attn.reshape(n_b * seq, d_model), w_o, h, TM_RESIDENT)


S5_BLOCK_GROUPS = LANES // S5_GROUP
S5_PAIR_ROWS = 2 * S5_GROUP
S5_PAIRS = S5_BLOCK_GROUPS // 2


def _s5_pair_tiles(w, n_lb):
    n_g, n_k, n_c, n_p = w.shape
    w = w.reshape(n_lb, S5_PAIRS, 2, n_k, n_c, n_p)
    even = jnp.pad(w[:, :, 0], ((0, 0),) * 4 + ((0, n_p),))
    odd = jnp.pad(w[:, :, 1], ((0, 0),) * 4 + ((n_p, 0),))
    return jnp.concatenate([even, odd], axis=3).astype(BF16)


def _s5_weights(a_re, a_im, log_dt, b_re, b_im, c_re, c_im, n_levels):
    n_g, n_p = a_re.shape
    n_lb = n_g // S5_BLOCK_GROUPS
    t_len = S5_CHUNK
    a = lax.complex(a_re.astype(F32), a_im.astype(F32))
    dt = jnp.exp(log_dt.astype(F32))[:, None]
    a_dt = a * dt
    a_bar = jnp.exp(a_dt)
    b_bar = ((a_bar - 1.0) / a)[..., None] * lax.complex(b_re.astype(F32), b_im.astype(F32))
    c_mat = lax.complex(c_re.astype(F32), c_im.astype(F32))
    steps = jnp.arange(t_len + 1, dtype=F32)
    a_pow = jnp.exp(a_dt[:, None, :] * steps[None, :, None])

    inp_c = a_pow[:, t_len - 1::-1][:, :t_len, None, :] * b_bar.transpose(0, 2, 1)[:, None]
    out_c = c_mat[:, None] * a_pow[:, :, None, :]
    w_in = (_s5_pair_tiles(inp_c.real, n_lb), _s5_pair_tiles(inp_c.imag, n_lb))
    w_out = (_s5_pair_tiles(out_c.real, n_lb), _s5_pair_tiles(-out_c.imag, n_lb))

    jumps = (t_len * 2 ** jnp.arange(n_levels)).astype(F32)
    d = jnp.exp(a_dt[:, None, :] * jumps[None, :, None])
    d = d.reshape(n_lb, S5_BLOCK_GROUPS, n_levels, n_p).transpose(0, 2, 1, 3)
    d = d.reshape(n_lb, n_levels, S5_BLOCK_GROUPS * n_p)
    decay = jnp.stack([d.real, d.imag], axis=2)
    return w_in, w_out, decay


def _s5_scan(u, w_in, w_out, decay, d_skip, n_chunks):
    t_len, n_rows, d_model = u.shape
    n_levels = decay.shape[1]
    half = decay.shape[3]
    wide = t_len * LANES

    def body(u_ref, wre_ref, wim_ref, vre_ref, vim_ref, dec_ref, d_ref,
             z_ref, a_scr, conv_scr, win_scr, vpow_scr):
        @pl.when(pl.program_id(0) == 0)
        def _():
            conv_scr[...] = jnp.zeros_like(conv_scr)
            win_scr[...] = jnp.zeros_like(win_scr)
            vpow_scr[...] = jnp.zeros_like(vpow_scr)

        def place(dst, k, re_ref, im_ref):
            for q in range(S5_PAIRS):
                rows = slice(k * LANES + q * S5_PAIR_ROWS, k * LANES + (q + 1) * S5_PAIR_ROWS)
                dst[rows, q * LANES:(q + 1) * LANES] = re_ref[0, q, k]
                dst[rows, half + q * LANES:half + (q + 1) * LANES] = im_ref[0, q, k]

        for t in range(t_len):
            a_scr[:, t * LANES:(t + 1) * LANES] = u_ref[t].astype(BF16)
            place(win_scr, t, wre_ref, wim_ref)
        for k in range(t_len + 1):
            place(vpow_scr, k, vre_ref, vim_ref)
        taps = _qk(win_scr[(t_len - 1) * LANES:, :], vpow_scr[:wide, :]).astype(BF16)
        for t in range(t_len):
            for t2 in range(t, t_len):
                conv_scr[t * LANES:(t + 1) * LANES, t2 * LANES:(t2 + 1) * LANES] = (
                    taps[:, (t2 - t) * LANES:(t2 - t + 1) * LANES])

        a = a_scr[...]
        x = jnp.dot(a, win_scr[...], preferred_element_type=F32)
        h_re, h_im = x[:, :half], x[:, half:]
        chunk = lax.broadcasted_iota(jnp.int32, (n_rows, half), 0) % n_chunks

        def shifted(v, n):
            return jnp.where(chunk >= n, pltpu.roll(v, n, 0), 0.0)

        for lvl in range(n_levels):
            p_re, p_im = shifted(h_re, 2 ** lvl), shifted(h_im, 2 ** lvl)
            d_re, d_im = dec_ref[0, lvl, 0:1, :], dec_ref[0, lvl, 1:2, :]
            h_re, h_im = (h_re + d_re * p_re - d_im * p_im,
                          h_im + d_re * p_im + d_im * p_re)
        h_in = jnp.concatenate([shifted(h_re, 1), shifted(h_im, 1)], axis=1).astype(BF16)
        y_state = _qk(h_in, vpow_scr[LANES:, :])
        n_split = 4
        per = t_len // n_split
        for part in range(n_split):
            k_hi = (part + 1) * per * LANES
            cols = slice(part * per * LANES, k_hi)
            y = y_state[:, cols] + jnp.dot(a_scr[:, :k_hi], conv_scr[:k_hi, cols],
                                           preferred_element_type=F32)
            for t in range(part * per, (part + 1) * per):
                lo = (t - part * per) * LANES
                z = y[:, lo:lo + LANES] + d_ref[...] * u_ref[t]
                z_ref[t] = jax.nn.gelu(z).astype(BF16)

    def pair_spec(n_k):
        return pl.BlockSpec((1, S5_PAIRS, n_k, S5_PAIR_ROWS, LANES),
                            lambda i: (i, 0, 0, 0, 0))

    return pl.pallas_call(
        body,
        grid=(d_model // LANES,),
        in_specs=[
            pl.BlockSpec((t_len, n_rows, LANES), lambda i: (0, 0, i)),
            pair_spec(t_len), pair_spec(t_len), pair_spec(t_len + 1), pair_spec(t_len + 1),
            pl.BlockSpec((1, n_levels, 2, half), lambda i: (i, 0, 0, 0)),
            pl.BlockSpec((1, LANES), lambda i: (0, i)),
        ],
        out_specs=pl.BlockSpec((t_len, n_rows, LANES), lambda i: (0, 0, i)),
        out_shape=jax.ShapeDtypeStruct((t_len, n_rows, d_model), BF16),
        scratch_shapes=[pltpu.VMEM((n_rows, wide), BF16),
                        pltpu.VMEM((wide, wide), BF16),
                        pltpu.VMEM((wide, 2 * half), BF16),
                        pltpu.VMEM((wide + LANES, 2 * half), BF16)],
        compiler_params=_compiler_params(("arbitrary",)),
    )(u, *w_in, *w_out, decay, d_skip)


def _s5_layer(h, gain, w_in, a_re, a_im, log_dt, b_re, b_im, c_re, c_im, d_skip,
              w_glu, n_b, seq, tm, tn):
    m_rows, d_model = h.shape
    t_len = S5_CHUNK
    n_chunks = seq // t_len
    n_levels = max(1, math.ceil(math.log2(n_chunks)))
    hp = h.reshape(n_b, n_chunks, t_len, d_model).transpose(2, 0, 1, 3).reshape(m_rows, d_model)
    u = _norm_proj_resident(hp, gain, w_in, TM_RESIDENT)
    s5_in, s5_out, decay = _s5_weights(a_re, a_im, log_dt, b_re, b_im, c_re, c_im, n_levels)
    z = _s5_scan(u.reshape(t_len, n_b * n_chunks, d_model), s5_in, s5_out, decay,
                 d_skip.astype(F32).reshape(1, d_model), n_chunks)

    def epilogue(accs, ex):
        return ex[0] + accs[0] * _sigmoid(accs[1])

    hp = _fused_matmul([(z.reshape(m_rows, d_model), ())], [], None, [],
                       [(0, *w_glu, 0), (0, *w_glu, d_model // tn)],
                       [(hp, 'tile')], epilogue, F32, d_model, tm, tn)
    return hp.reshape(t_len, n_b, n_chunks, d_model).transpose(1, 2, 0, 3).reshape(m_rows, d_model)


@jax.jit
def _trunk(x, p, norm_g, final_g, w_ffn_in, w_ffn_out, w_ple_gate, w_ple_proj,
           fox_w_in, fox_b_f, fox_w_o, moba_w_qkv, moba_w_o,
           s5_w_in, s5_a_re, s5_a_im, s5_log_dt, s5_b_re, s5_b_im,
           s5_c_re, s5_c_im, s5_d, s5_w_glu):
    n_b, seq, d_model = x.shape
    depth = norm_g.shape[0]
    m_rows = n_b * seq
    tm = min(1024, m_rows)
    tn = 512
    tn_ffn_out = 256
    h = x.reshape(m_rows, d_model)
    p_rows = p.reshape(depth, m_rows, p.shape[-1])
    for i in range(depth):
        g = norm_g[i]
        h = _ffn(h, g[0], (w_ffn_in, (i, 0)), (w_ffn_out, (i, 0)), tn, tn_ffn_out, tm)
        kind, j = i % 3, i // 3
        if kind == 0:
            h = _fox_layer(h, g[1], (fox_w_in, (j,)), fox_b_f[j], (fox_w_o, (j,)),
                           n_b, seq, tm, tn)
        elif kind == 1:
            h = _moba_layer(h, g[1], (moba_w_qkv, (j,)), (moba_w_o, (j,)), n_b, seq, tm, tn)
        else:
            h = _s5_layer(h, g[1], (s5_w_in, (j,)), s5_a_re[j], s5_a_im[j],
                          s5_log_dt[j], s5_b_re[j], s5_b_im[j], s5_c_re[j], s5_c_im[j],
                          s5_d[j], (s5_w_glu, (j,)), n_b, seq, tm, tn)
        h = _ffn(h, g[2], (w_ffn_in, (i, 1)), (w_ffn_out, (i, 1)), tn, tn_ffn_out, tm)
        h = _ple(h, g[3], (p_rows, (i,)), (w_ple_gate, (i,)), (w_ple_proj, (i,)),
                 TM_RESIDENT, final_gain=final_g if i == depth - 1 else None)
    return h.reshape(n_b, seq, d_model)


def kernel(x, p, norm_g, final_g, w_ffn_in, w_ffn_out, w_ple_gate, w_ple_proj, fox_w_in, fox_b_f, fox_w_o, moba_w_qkv, moba_w_o, s5_w_in, s5_a_re, s5_a_im, s5_log_dt, s5_b_re, s5_b_im, s5_c_re, s5_c_im, s5_d, s5_w_glu):
    return _trunk(x, p, norm_g, final_g, w_ffn_in, w_ffn_out, w_ple_gate, w_ple_proj,
                  fox_w_in, fox_b_f, fox_w_o, moba_w_qkv, moba_w_o,
                  s5_w_in, s5_a_re, s5_a_im, s5_log_dt, s5_b_re, s5_b_im,
                  s5_c_re, s5_c_im, s5_d, s5_w_glu)
```

```python
import functools
import math

import jax
import jax.numpy as jnp
from jax import lax
from jax.experimental import pallas as pl
from jax.experimental.pallas import tpu as pltpu

F32 = jnp.float32
BF16 = jnp.bfloat16

N_HEADS = 16
HEAD_DIM = 128
LANES = 128
SUBLANES = 8
ATTN_BLOCK = 256
MOBA_BLOCK = 256
MOBA_TOP_K = 3
S5_GROUP = 16
S5_STATE = 64
S5_CHUNK = 16
RMS_EPS = 1e-6
MASK_VALUE = -1e30
LOG2E = math.log2(math.e)
Q_PRESCALE = HEAD_DIM ** -0.5 * LOG2E

V7X_VMEM_BYTES = 64 * 1024 * 1024
VMEM_LIMIT_BYTES = V7X_VMEM_BYTES - 8 * 1024 * 1024
PROLOGUE_ROWS = 256
TM_RESIDENT = 256
WEIGHT_CHUNK_ROWS = 512
GLU_CHUNK_ROWS = 256
QKV_CHUNK_ROWS = 128


def _compiler_params(semantics):
    return pltpu.CompilerParams(dimension_semantics=semantics,
                                vmem_limit_bytes=VMEM_LIMIT_BYTES)


def _sigmoid(x):
    return 1.0 / (1.0 + jnp.exp(-x))


def _rms_rows(x, g):
    ms = jnp.mean(x * x, axis=-1, keepdims=True)
    return x * lax.rsqrt(ms + RMS_EPS) * g


def _prefixed_spec(prefix, block, index_map):
    return pl.BlockSpec((None,) * len(prefix) + block,
                        lambda *ids: prefix + index_map(*ids))


def _fused_matmul(a_list, row_consts, prologue, lhs_widths, terms, extras,
                  epilogue, out_dtype, n_out, tm, tn, valid_cols=None):
    m_rows = a_list[0][0].shape[-2]
    grid = (m_rows // tm, n_out // tn)
    n_a, n_c, n_t, n_e = len(a_list), len(row_consts), len(terms), len(extras)

    def body(*refs):
        a_refs = refs[:n_a]
        c_refs = refs[n_a:n_a + n_c]
        w_refs = refs[n_a + n_c:n_a + n_c + n_t]
        e_refs = refs[n_a + n_c + n_t:n_a + n_c + n_t + n_e]
        o_ref = refs[n_a + n_c + n_t + n_e]
        lhs_refs = refs[n_a + n_c + n_t + n_e + 1:]
        if prologue is not None:
            @pl.when(pl.program_id(1) == 0)
            def _():
                consts = [c[...] for c in c_refs]

                def rows_step(r, carry):
                    rows = pl.ds(pl.multiple_of(r * PROLOGUE_ROWS, PROLOGUE_ROWS),
                                 PROLOGUE_ROWS)
                    outs = prologue([a[rows, :] for a in a_refs], consts)
                    for dst, val in zip(lhs_refs, outs):
                        dst[rows, :] = val
                    return carry

                lax.fori_loop(0, tm // PROLOGUE_ROWS, rows_step, 0)
        else:
            lhs_refs = a_refs
        def weight_tile(w_ref):
            w = w_ref[...]
            if valid_cols is not None:
                lane = lax.broadcasted_iota(jnp.int32, w.shape, 1)
                w = jnp.where(lane < valid_cols, w, 0.0)
            return w.astype(BF16)

        accs = [jnp.dot(lhs_refs[li][...], weight_tile(w), preferred_element_type=F32)
                for (li, _, _, _), w in zip(terms, w_refs)]
        o_ref[...] = epilogue(accs, [e[...] for e in e_refs]).astype(out_dtype)

    in_specs = [_prefixed_spec(pre, (tm, a.shape[-1]), lambda i, j: (i, 0))
                for a, pre in a_list]
    in_specs += [pl.BlockSpec((1, c.shape[1]), lambda i, j: (0, 0)) for c in row_consts]
    for _, w, pre, off in terms:
        in_specs.append(_prefixed_spec(
            pre, (w.shape[-2], tn), functools.partial(lambda i, j, o: (0, j + o), o=off)))
    for arr, kind in extras:
        if kind == 'tile':
            in_specs.append(pl.BlockSpec((tm, tn), lambda i, j: (i, j)))
        else:
            in_specs.append(pl.BlockSpec((1, tn), lambda i, j: (0, j)))
    scratch = []
    if prologue is not None:
        scratch = [pltpu.VMEM((tm, k), BF16) for k in lhs_widths]
    return pl.pallas_call(
        body,
        grid=grid,
        in_specs=in_specs,
        out_specs=pl.BlockSpec((tm, tn), lambda i, j: (i, j)),
        out_shape=jax.ShapeDtypeStruct((m_rows, n_out), out_dtype),
        scratch_shapes=scratch,
        compiler_params=_compiler_params(("parallel", "arbitrary")),
    )(*[a for a, _ in a_list], *row_consts, *[w for _, w, _, _ in terms],
      *[e for e, _ in extras])


def _norm_prologue(a_tiles, consts):
    return [_rms_rows(a_tiles[0], consts[0]).astype(BF16)]


def _resident_matmul(a_list, row_consts, prologue, terms, epilogue, outs, tm, n_stage):
    m_rows = a_list[0][0].shape[-2]
    n_a, n_c, n_t, n_o = len(a_list), len(row_consts), len(terms), len(outs)
    padded = [-(-w.shape[-1] // LANES) * LANES for _, w, _ in terms]

    def body(*refs):
        a_refs = refs[:n_a]
        c_refs = refs[n_a:n_a + n_c]
        w_refs = refs[n_a + n_c:n_a + n_c + n_t]
        o_refs = refs[n_a + n_c + n_t:n_a + n_c + n_t + n_o]
        wb_refs = refs[n_a + n_c + n_t + n_o:]
        g = pl.program_id(0)

        @pl.when(g == 0)
        def _():
            for (_, w, _), wb_ref, n_pad in zip(terms, wb_refs, padded):
                if n_pad != w.shape[-1]:
                    wb_ref[:, n_pad - LANES:] = jnp.zeros((w.shape[-2], LANES), BF16)

        @pl.when(g < n_stage)
        def _():
            for (_, w, _), w_ref, wb_ref in zip(terms, w_refs, wb_refs):
                chunk = w.shape[-2] // n_stage
                rows = pl.ds(pl.multiple_of(g * chunk, chunk), chunk)
                wb_ref[rows, :w.shape[-1]] = w_ref[...].astype(BF16)

        @pl.when(g >= n_stage)
        def _():
            a_tiles = [a[...] for a in a_refs]
            consts = [c[...] for c in c_refs]
            lhs = prologue(a_tiles, consts)
            accs = [jnp.dot(lhs[li], wb[...], preferred_element_type=F32)
                    for (li, _, _), wb in zip(terms, wb_refs)]
            for o_ref, tile, (_, dtype) in zip(o_refs, epilogue(accs, a_tiles, consts), outs):
                o_ref[...] = tile.astype(dtype)

    def row_tile(g):
        return (jnp.maximum(g - n_stage, 0), 0)

    in_specs = [_prefixed_spec(pre, (tm, a.shape[-1]), row_tile) for a, pre in a_list]
    in_specs += [pl.BlockSpec((1, c.shape[1]), lambda g: (0, 0)) for c in row_consts]
    for _, w, pre in terms:
        in_specs.append(_prefixed_spec(pre, (w.shape[-2] // n_stage, w.shape[-1]),
                                       lambda g: (jnp.minimum(g, n_stage - 1), 0)))
    results = pl.pallas_call(
        body,
        grid=(n_stage + m_rows // tm,),
        in_specs=in_specs,
        out_specs=[pl.BlockSpec((tm, n), row_tile) for n, _ in outs],
        out_shape=[jax.ShapeDtypeStruct((m_rows, n), dtype) for n, dtype in outs],
        scratch_shapes=[pltpu.VMEM((w.shape[-2], n_pad), BF16)
                        for (_, w, _), n_pad in zip(terms, padded)],
        compiler_params=_compiler_params(("arbitrary",)),
    )(*[a for a, _ in a_list], *row_consts, *[w for _, w, _ in terms])
    return results


def _ffn(h, gain, w_in, w_out, tm, tn_in):
    d_model = h.shape[1]
    d_ff = w_out[0].shape[-2]

    def act_epilogue(accs, _):
        gate, up = accs
        return gate * _sigmoid(gate) * up

    act = _fused_matmul([(h, ())], [gain.reshape(1, d_model)], _norm_prologue, [d_model],
                        [(0, *w_in, 0), (0, *w_in, d_ff // tn_in)], [],
                        act_epilogue, BF16, d_ff, tm, tn_in)

    (out,) = _resident_matmul([(act, ()), (h, ())], [], lambda a_tiles, _: [a_tiles[0]],
                              [(0, *w_out)],
                              lambda accs, a_tiles, _: [a_tiles[1] + 0.5 * accs[0]],
                              [(d_model, F32)], TM_RESIDENT, d_ff // WEIGHT_CHUNK_ROWS)
    return out


def _ple(h, gain, p_i, w_gate, w_proj, tm, final_gain=None):
    d_model = h.shape[1]
    consts = [gain.reshape(1, d_model)]
    if final_gain is not None:
        consts.append(final_gain.reshape(1, d_model))

    def prologue(a_tiles, consts):
        return [_rms_rows(a_tiles[0], consts[0]).astype(BF16), a_tiles[1].astype(BF16)]

    def epilogue(accs, a_tiles, consts):
        out = a_tiles[0] + _sigmoid(accs[0]) * accs[1]
        return [out if final_gain is None else _rms_rows(out, consts[1])]

    (out,) = _resident_matmul([(h, ()), p_i], consts, prologue,
                              [(0, *w_gate), (1, *w_proj)], epilogue, [(d_model, F32)],
                              tm, d_model // WEIGHT_CHUNK_ROWS)
    return out


def _norm_proj(h, gain, w, out_dtype, tm, tn, n_out, q_blocks=0):
    d_model = h.shape[1]

    def epilogue(accs, _):
        if not q_blocks:
            return accs[0]
        return accs[0] * jnp.where(pl.program_id(1) < q_blocks, Q_PRESCALE, 1.0)

    return _fused_matmul([(h, ())], [gain.reshape(1, d_model)], _norm_prologue, [d_model],
                         [(0, *w, 0)], [], epilogue, out_dtype, n_out, tm, tn)


def _out_proj(a, w, h, tm):
    (out,) = _resident_matmul([(a, ()), (h, ())], [], lambda a_tiles, _: [a_tiles[0]],
                              [(0, *w)], lambda accs, a_tiles, _: [a_tiles[1] + accs[0]],
                              [(h.shape[1], F32)], tm, h.shape[1] // WEIGHT_CHUNK_ROWS)
    return out


def _norm_proj_resident(h, gain, w, tm):
    d_model = h.shape[1]
    (out,) = _resident_matmul([(h, ())], [gain.reshape(1, d_model)], _norm_prologue,
                              [(0, *w)], lambda accs, _a, _c: [accs[0]],
                              [(w[0].shape[-1], F32)], tm, d_model // WEIGHT_CHUNK_ROWS)
    return out


def _qkv_proj(h, gain, w, tm, forget_bias=None):
    d_model = h.shape[1]
    consts = [gain.reshape(1, d_model)]
    outs = [(3 * d_model, BF16)]
    if forget_bias is not None:
        consts.append(forget_bias)
        outs.append((LANES, F32))

    def epilogue(accs, _a, consts):
        acc = accs[0]
        qkv = jnp.concatenate([acc[:, :d_model] * Q_PRESCALE, acc[:, d_model:3 * d_model]],
                              axis=1)
        if forget_bias is None:
            return [qkv]
        z = acc[:, 3 * d_model:] + consts[1]
        return [qkv, jnp.minimum(z, 0.0) - jnp.log(1.0 + jnp.exp(-jnp.abs(z)))]

    return _resident_matmul([(h, ())], consts, _norm_prologue, [(0, *w)], epilogue, outs,
                            tm, d_model // QKV_CHUNK_ROWS)


def _qk(q, k):
    return lax.dot_general(q, k, (((1,), (1,)), ((), ())), preferred_element_type=F32)


def _bf16_pieces(x):
    hi = x.astype(BF16).astype(F32)
    rest = x - hi
    mid = rest.astype(BF16).astype(F32)
    return hi, mid, rest - mid


def _lane_table(lane, entries, default=0.0):
    out = default
    for idx, val in reversed(list(entries.items())):
        out = jnp.where(lane == idx, val, out)
    return out


def _attend_all_tiles(q_ref, qaug_ref, kaug_ref, v_ref, o_ref, n_blk):
    blk = ATTN_BLOCK

    def logits(qi):
        rows = slice(qi * blk, (qi + 1) * blk)
        q_full = jnp.concatenate([q_ref[0, rows, :], qaug_ref[rows, :]], axis=1)
        return _qk(q_full, kaug_ref[0:(qi + 1) * blk, :])

    s = logits(0)
    for qi in range(n_blk):
        s_next = logits(qi + 1) if qi + 1 < n_blk else None
        o_ref[0, qi * blk:(qi + 1) * blk, :] = _softmax_pv(s, v_ref, qi + 1).astype(BF16)
        s = s_next


def _softmax_pv(s, v_ref, n_blocks):
    blk = ATTN_BLOCK
    n_keys = n_blocks * blk
    row = lax.broadcasted_iota(jnp.int32, (blk, blk), 0)
    col = lax.broadcasted_iota(jnp.int32, (blk, blk), 1)
    diag = jnp.where(row >= col, s[:, n_keys - blk:], MASK_VALUE)
    parts = [s[:, i * LANES:(i + 1) * LANES] for i in range((n_keys - blk) // LANES)]
    parts += [diag[:, :LANES], diag[:, LANES:]]
    m = jnp.max(functools.reduce(jnp.maximum, parts), axis=1, keepdims=True)
    probs = [jnp.exp2(part - m) for part in parts]
    denom = jnp.sum(functools.reduce(jnp.add, probs), axis=1, keepdims=True)
    p = jnp.concatenate([pr.astype(BF16) for pr in probs], axis=1)
    acc = jnp.dot(p, v_ref[0, 0:n_keys, :], preferred_element_type=F32)
    return acc / denom


def _log_forget_cumsum(logf):
    n_b, seq, lanes = logf.shape

    def body(x_ref, o_ref):
        x = x_ref[0]
        row = lax.broadcasted_iota(jnp.int32, (seq, lanes), 0)
        shift = 1
        while shift < seq:
            x = x + jnp.where(row >= shift, pltpu.roll(x, shift, 0), 0.0)
            shift *= 2
        o_ref[0] = x

    return pl.pallas_call(
        body,
        grid=(n_b,),
        in_specs=[pl.BlockSpec((1, seq, lanes), lambda b: (b, 0, 0))],
        out_specs=pl.BlockSpec((1, seq, lanes), lambda b: (b, 0, 0)),
        out_shape=jax.ShapeDtypeStruct((n_b, seq, lanes), F32),
        compiler_params=_compiler_params(("parallel",)),
    )(logf)


def _fox_attention(qkv, c):
    n_b, seq, _ = qkv.shape
    blk = ATTN_BLOCK
    n_blk = seq // blk

    def body(q_ref, k_ref, v_ref, c_ref, o_ref, kaug_ref, qaug_ref):
        h = pl.program_id(1)

        for r in range(n_blk):
            rows = slice(r * blk, (r + 1) * blk)
            lane = lax.broadcasted_iota(jnp.int32, (blk, LANES), 1)
            c_head = jnp.sum(jnp.where(lane == h, c_ref[0, rows, :], 0.0),
                             axis=1, keepdims=True) * LOG2E
            hi, mid, lo = _bf16_pieces(c_head)
            q_aug = _lane_table(lane, {0: hi, 1: mid, 2: lo, 3: 1.0, 4: 1.0, 5: 1.0})
            k_aug = _lane_table(lane, {0: 1.0, 1: 1.0, 2: 1.0, 3: -hi, 4: -mid, 5: -lo})
            qaug_ref[rows, :] = q_aug.astype(BF16)
            kaug_ref[rows, :LANES] = k_ref[0, rows, :]
            kaug_ref[rows, LANES:] = k_aug.astype(BF16)
        _attend_all_tiles(q_ref, qaug_ref, kaug_ref, v_ref, o_ref, n_blk)

    return pl.pallas_call(
        body,
        grid=(n_b, N_HEADS),
        in_specs=[
            pl.BlockSpec((1, seq, HEAD_DIM), lambda b, h: (b, 0, h)),
            pl.BlockSpec((1, seq, HEAD_DIM), lambda b, h: (b, 0, N_HEADS + h)),
            pl.BlockSpec((1, seq, HEAD_DIM), lambda b, h: (b, 0, 2 * N_HEADS + h)),
            pl.BlockSpec((1, seq, LANES), lambda b, h: (b, 0, 0)),
        ],
        out_specs=pl.BlockSpec((1, seq, HEAD_DIM), lambda b, h: (b, 0, h)),
        out_shape=jax.ShapeDtypeStruct((n_b, seq, N_HEADS * HEAD_DIM), BF16),
        scratch_shapes=[pltpu.VMEM((seq, 2 * LANES), BF16),
                        pltpu.VMEM((seq, LANES), BF16)],
        compiler_params=_compiler_params(("parallel", "arbitrary")),
    )(qkv, qkv, qkv, c)


def _fox_layer(h, gain, w_in, b_f, w_o, n_b, seq):
    d_model = h.shape[1]
    b_row = jnp.pad(b_f, (0, LANES - N_HEADS)).reshape(1, LANES)
    qkv, logf = _qkv_proj(h, gain, w_in, TM_RESIDENT, forget_bias=b_row)
    c = _log_forget_cumsum(logf.reshape(n_b, seq, LANES))
    attn = _fox_attention(qkv.reshape(n_b, seq, 3 * d_model), c)
    return _out_proj(attn.reshape(n_b * seq, d_model), w_o, h, TM_RESIDENT)


def _moba_attention(qkv, slopes):
    n_b, seq, _ = qkv.shape
    blk = MOBA_BLOCK
    assert blk == ATTN_BLOCK
    n_blk = seq // blk
    assert n_blk <= SUBLANES
    k_lanes = [SUBLANES, SUBLANES + 1, SUBLANES + 2]
    q_lanes = [SUBLANES + 3, SUBLANES + 4, SUBLANES + 5]

    def body(slopes_ref, q_ref, k_ref, v_ref, o_ref, kaug_ref, qaug_ref, kmean_ref):
        h = pl.program_id(1)
        slope2 = slopes_ref[h] * LOG2E

        kmean_ref[...] = jnp.zeros_like(kmean_ref)
        for n in range(n_blk):
            rows = slice(n * blk, (n + 1) * blk)
            kb = k_ref[0, rows, :]
            kmean_ref[n:n + 1, :] = jnp.mean(kb.astype(F32), axis=0, keepdims=True)
            lane = lax.broadcasted_iota(jnp.int32, (blk, LANES), 1)
            pos = (lax.broadcasted_iota(jnp.int32, (blk, 1), 0) + n * blk).astype(F32)
            hi, mid, lo = _bf16_pieces(slope2 * pos)
            entries = {n: 1.0, k_lanes[0]: hi, k_lanes[1]: mid, k_lanes[2]: lo,
                       q_lanes[0]: 1.0, q_lanes[1]: 1.0, q_lanes[2]: 1.0}
            kaug_ref[rows, :LANES] = kb
            kaug_ref[rows, LANES:] = _lane_table(lane, entries).astype(BF16)

        gate = _qk(kmean_ref[...].astype(BF16), q_ref[0])[:SUBLANES]
        key_blk = lax.broadcasted_iota(jnp.int32, (SUBLANES, seq), 0)
        q_pos = lax.broadcasted_iota(jnp.int32, (SUBLANES, seq), 1)
        q_blk = q_pos // blk
        beaten = jnp.zeros((SUBLANES, seq), F32)
        for n2 in range(n_blk):
            rival = gate[n2:n2 + 1, :]
            wins = jnp.where(key_blk > n2, jnp.where(rival >= gate, 1.0, 0.0),
                             jnp.where(rival > gate, 1.0, 0.0))
            beaten = beaten + jnp.where(n2 < q_blk, wins, 0.0)
        bias = jnp.where(key_blk < q_blk,
                         jnp.where(beaten < MOBA_TOP_K, 0.0, MASK_VALUE),
                         jnp.where(key_blk == q_blk, 0.0, MASK_VALUE))
        hi, mid, lo = _bf16_pieces(-slope2 * q_pos.astype(F32))
        pieces = jnp.where(key_blk < 3, 1.0,
                           jnp.where(key_blk == 3, hi,
                                     jnp.where(key_blk == 4, mid,
                                               jnp.where(key_blk == 5, lo, 0.0))))
        aug_t = jnp.concatenate(
            [bias, pieces, jnp.zeros((LANES - 2 * SUBLANES, seq), F32)], axis=0)
        for qi in range(n_blk):
            rows = slice(qi * blk, (qi + 1) * blk)
            qaug_ref[rows, :] = aug_t[:, rows].T.astype(BF16)
        _attend_all_tiles(q_ref, qaug_ref, kaug_ref, v_ref, o_ref, n_blk)

    return pl.pallas_call(
        body,
        grid=(n_b, N_HEADS),
        in_specs=[
            pl.BlockSpec(memory_space=pltpu.SMEM),
            pl.BlockSpec((1, seq, HEAD_DIM), lambda b, h: (b, 0, h)),
            pl.BlockSpec((1, seq, HEAD_DIM), lambda b, h: (b, 0, N_HEADS + h)),
            pl.BlockSpec((1, seq, HEAD_DIM), lambda b, h: (b, 0, 2 * N_HEADS + h)),
        ],
        out_specs=pl.BlockSpec((1, seq, HEAD_DIM), lambda b, h: (b, 0, h)),
        out_shape=jax.ShapeDtypeStruct((n_b, seq, N_HEADS * HEAD_DIM), BF16),
        scratch_shapes=[pltpu.VMEM((seq, 2 * LANES), BF16),
                        pltpu.VMEM((seq, LANES), BF16),
                        pltpu.VMEM((2 * SUBLANES, HEAD_DIM), F32)],
        compiler_params=_compiler_params(("parallel", "arbitrary")),
    )(slopes, qkv, qkv, qkv)


def _moba_layer(h, gain, w_qkv, w_o, n_b, seq):
    d_model = h.shape[1]
    (qkv,) = _qkv_proj(h, gain, w_qkv, TM_RESIDENT)
    slopes = 2.0 ** (-8.0 * jnp.arange(1, N_HEADS + 1, dtype=F32) / N_HEADS)
    attn = _moba_attention(qkv.reshape(n_b, seq, 3 * d_model), slopes)
    return _out_proj(attn.reshape(n_b * seq, d_model), w_o, h, TM_RESIDENT)


S5_BLOCK_GROUPS = LANES // S5_GROUP
S5_PAIR_ROWS = 2 * S5_GROUP
S5_PAIRS = S5_BLOCK_GROUPS // 2


def _s5_pair_tiles(w, n_lb):
    n_g, n_k, n_c, n_p = w.shape
    w = w.reshape(n_lb, S5_PAIRS, 2, n_k, n_c, n_p)
    even = jnp.pad(w[:, :, 0], ((0, 0),) * 4 + ((0, n_p),))
    odd = jnp.pad(w[:, :, 1], ((0, 0),) * 4 + ((n_p, 0),))
    return jnp.concatenate([even, odd], axis=3).astype(BF16)


def _s5_weights(a_re, a_im, log_dt, b_re, b_im, c_re, c_im, n_levels):
    n_g, n_p = a_re.shape
    n_lb = n_g // S5_BLOCK_GROUPS
    t_len = S5_CHUNK
    a = lax.complex(a_re.astype(F32), a_im.astype(F32))
    dt = jnp.exp(log_dt.astype(F32))[:, None]
    a_dt = a * dt
    a_bar = jnp.exp(a_dt)
    b_bar = ((a_bar - 1.0) / a)[..., None] * lax.complex(b_re.astype(F32), b_im.astype(F32))
    c_mat = lax.complex(c_re.astype(F32), c_im.astype(F32))
    steps = jnp.arange(t_len + 1, dtype=F32)
    a_pow = jnp.exp(a_dt[:, None, :] * steps[None, :, None])

    inp_c = a_pow[:, t_len - 1::-1][:, :t_len, None, :] * b_bar.transpose(0, 2, 1)[:, None]
    out_c = c_mat[:, None] * a_pow[:, :, None, :]
    w_in = (_s5_pair_tiles(inp_c.real, n_lb), _s5_pair_tiles(inp_c.imag, n_lb))
    w_out = (_s5_pair_tiles(out_c.real, n_lb), _s5_pair_tiles(-out_c.imag, n_lb))

    jumps = (t_len * 2 ** jnp.arange(n_levels)).astype(F32)
    d = jnp.exp(a_dt[:, None, :] * jumps[None, :, None])
    d = d.reshape(n_lb, S5_BLOCK_GROUPS, n_levels, n_p).transpose(0, 2, 1, 3)
    d = d.reshape(n_lb, n_levels, S5_BLOCK_GROUPS * n_p)
    decay = jnp.stack([d.real, d.imag], axis=2)
    return w_in, w_out, decay


def _s5_scan(u, w_in, w_out, decay, d_skip, n_chunks):
    t_len, n_rows, d_model = u.shape
    n_levels = decay.shape[1]
    half = decay.shape[3]
    wide = t_len * LANES

    def body(u_ref, wre_ref, wim_ref, vre_ref, vim_ref, dec_ref, d_ref,
             z_ref, a_scr, conv_scr, win_scr, vpow_scr):
        @pl.when(pl.program_id(0) == 0)
        def _():
            conv_scr[...] = jnp.zeros_like(conv_scr)
            win_scr[...] = jnp.zeros_like(win_scr)
            vpow_scr[...] = jnp.zeros_like(vpow_scr)

        def place(dst, k, re_ref, im_ref):
            for q in range(S5_PAIRS):
                rows = slice(k * LANES + q * S5_PAIR_ROWS, k * LANES + (q + 1) * S5_PAIR_ROWS)
                dst[rows, q * LANES:(q + 1) * LANES] = re_ref[0, q, k]
                dst[rows, half + q * LANES:half + (q + 1) * LANES] = im_ref[0, q, k]

        for t in range(t_len):
            a_scr[:, t * LANES:(t + 1) * LANES] = u_ref[t].astype(BF16)
            place(win_scr, t, wre_ref, wim_ref)
        for k in range(t_len + 1):
            place(vpow_scr, k, vre_ref, vim_ref)
        taps = _qk(win_scr[(t_len - 1) * LANES:, :], vpow_scr[:wide, :]).astype(BF16)
        for t in range(t_len):
            for t2 in range(t, t_len):
                conv_scr[t * LANES:(t + 1) * LANES, t2 * LANES:(t2 + 1) * LANES] = (
                    taps[:, (t2 - t) * LANES:(t2 - t + 1) * LANES])

        a = a_scr[...]
        x = jnp.dot(a, win_scr[...], preferred_element_type=F32)
        h_re, h_im = x[:, :half], x[:, half:]
        chunk = lax.broadcasted_iota(jnp.int32, (n_rows, half), 0) % n_chunks

        def shifted(v, n):
            return jnp.where(chunk >= n, pltpu.roll(v, n, 0), 0.0)

        for lvl in range(n_levels):
            p_re, p_im = shifted(h_re, 2 ** lvl), shifted(h_im, 2 ** lvl)
            d_re, d_im = dec_ref[0, lvl, 0:1, :], dec_ref[0, lvl, 1:2, :]
            h_re, h_im = (h_re + d_re * p_re - d_im * p_im,
                          h_im + d_re * p_im + d_im * p_re)
        h_in = jnp.concatenate([shifted(h_re, 1), shifted(h_im, 1)], axis=1).astype(BF16)
        y_state = _qk(h_in, vpow_scr[LANES:, :])
        n_split = 4
        per = t_len // n_split
        for part in range(n_split):
            k_hi = (part + 1) * per * LANES
            cols = slice(part * per * LANES, k_hi)
            y = y_state[:, cols] + jnp.dot(a_scr[:, :k_hi], conv_scr[:k_hi, cols],
                                           preferred_element_type=F32)
            for t in range(part * per, (part + 1) * per):
                lo = (t - part * per) * LANES
                z = y[:, lo:lo + LANES] + d_ref[...] * u_ref[t]
                z_ref[t] = jax.nn.gelu(z).astype(BF16)

    def pair_spec(n_k):
        return pl.BlockSpec((1, S5_PAIRS, n_k, S5_PAIR_ROWS, LANES),
                            lambda i: (i, 0, 0, 0, 0))

    return pl.pallas_call(
        body,
        grid=(d_model // LANES,),
        in_specs=[
            pl.BlockSpec((t_len, n_rows, LANES), lambda i: (0, 0, i)),
            pair_spec(t_len), pair_spec(t_len), pair_spec(t_len + 1), pair_spec(t_len + 1),
            pl.BlockSpec((1, n_levels, 2, half), lambda i: (i, 0, 0, 0)),
            pl.BlockSpec((1, LANES), lambda i: (0, i)),
        ],
        out_specs=pl.BlockSpec((t_len, n_rows, LANES), lambda i: (0, 0, i)),
        out_shape=jax.ShapeDtypeStruct((t_len, n_rows, d_model), BF16),
        scratch_shapes=[pltpu.VMEM((n_rows, wide), BF16),
                        pltpu.VMEM((wide, wide), BF16),
                        pltpu.VMEM((wide, 2 * half), BF16),
                        pltpu.VMEM((wide + LANES, 2 * half), BF16)],
        compiler_params=_compiler_params(("arbitrary",)),
    )(u, *w_in, *w_out, decay, d_skip)


def _s5_layer(h, gain, w_in, a_re, a_im, log_dt, b_re, b_im, c_re, c_im, d_skip,
              w_glu, n_b, seq):
    m_rows, d_model = h.shape
    t_len = S5_CHUNK
    n_chunks = seq // t_len
    n_levels = max(1, math.ceil(math.log2(n_chunks)))
    hp = h.reshape(n_b, n_chunks, t_len, d_model).transpose(2, 0, 1, 3).reshape(m_rows, d_model)
    u = _norm_proj_resident(hp, gain, w_in, TM_RESIDENT)
    s5_in, s5_out, decay = _s5_weights(a_re, a_im, log_dt, b_re, b_im, c_re, c_im, n_levels)
    z = _s5_scan(u.reshape(t_len, n_b * n_chunks, d_model), s5_in, s5_out, decay,
                 d_skip.astype(F32).reshape(1, d_model), n_chunks)

    def glu_epilogue(accs, a_tiles, _):
        return [a_tiles[1] + accs[0][:, :d_model] * _sigmoid(accs[0][:, d_model:])]

    (hp,) = _resident_matmul([(z.reshape(m_rows, d_model), ()), (hp, ())], [],
                             lambda a_tiles, _: [a_tiles[0]], [(0, *w_glu)], glu_epilogue,
                             [(d_model, F32)], TM_RESIDENT, d_model // GLU_CHUNK_ROWS)
    return hp.reshape(t_len, n_b, n_chunks, d_model).transpose(1, 2, 0, 3).reshape(m_rows, d_model)


@jax.jit
def _trunk(x, p, norm_g, final_g, w_ffn_in, w_ffn_out, w_ple_gate, w_ple_proj,
           fox_w_in, fox_b_f, fox_w_o, moba_w_qkv, moba_w_o,
           s5_w_in, s5_a_re, s5_a_im, s5_log_dt, s5_b_re, s5_b_im,
           s5_c_re, s5_c_im, s5_d, s5_w_glu):
    n_b, seq, d_model = x.shape
    depth = norm_g.shape[0]
    m_rows = n_b * seq
    tm = min(1024, m_rows)
    tn = 512
    h = x.reshape(m_rows, d_model)
    p_rows = p.reshape(depth, m_rows, p.shape[-1])
    for i in range(depth):
        g = norm_g[i]
        h = _ffn(h, g[0], (w_ffn_in, (i, 0)), (w_ffn_out, (i, 0)), tm, tn)
        kind, j = i % 3, i // 3
        if kind == 0:
            h = _fox_layer(h, g[1], (fox_w_in, (j,)), fox_b_f[j], (fox_w_o, (j,)), n_b, seq)
        elif kind == 1:
            h = _moba_layer(h, g[1], (moba_w_qkv, (j,)), (moba_w_o, (j,)), n_b, seq)
        else:
            h = _s5_layer(h, g[1], (s5_w_in, (j,)), s5_a_re[j], s5_a_im[j],
                          s5_log_dt[j], s5_b_re[j], s5_b_im[j], s5_c_re[j], s5_c_im[j],
                          s5_d[j], (s5_w_glu, (j,)), n_b, seq)
        h = _ffn(h, g[2], (w_ffn_in, (i, 1)), (w_ffn_out, (i, 1)), tm, tn)
        h = _ple(h, g[3], (p_rows, (i,)), (w_ple_gate, (i,)), (w_ple_proj, (i,)),
                 TM_RESIDENT, final_gain=final_g if i == depth - 1 else None)
    return h.reshape(n_b, seq, d_model)


def kernel(x, p, norm_g, final_g, w_ffn_in, w_ffn_out, w_ple_gate, w_ple_proj, fox_w_in, fox_b_f, fox_w_o, moba_w_qkv, moba_w_o, s5_w_in, s5_a_re, s5_a_im, s5_log_dt, s5_b_re, s5_b_im, s5_c_re, s5_c_im, s5_d, s5_w_glu):
    return _trunk(x, p, norm_g, final_g, w_ffn_in, w_ffn_out, w_ple_gate, w_ple_proj,
                  fox_w_in, fox_b_f, fox_w_o, moba_w_qkv, moba_w_o,
                  s5_w_in, s5_a_re, s5_a_im, s5_log_dt, s5_b_re, s5_b_im,
                  s5_c_re, s5_c_im, s5_d, s5_w_glu)
```

```python
import functools
import math

import jax
import jax.numpy as jnp
from jax import lax
from jax.experimental import pallas as pl
from jax.experimental.pallas import tpu as pltpu

F32 = jnp.float32
BF16 = jnp.bfloat16

N_HEADS = 16
HEAD_DIM = 128
LANES = 128
SUBLANES = 8
ATTN_BLOCK = 256
MOBA_BLOCK = 256
MOBA_TOP_K = 3
S5_GROUP = 16
S5_STATE = 64
S5_CHUNK = 16
RMS_EPS = 1e-6
MASK_VALUE = -1e30
LOG2E = math.log2(math.e)
Q_PRESCALE = HEAD_DIM ** -0.5 * LOG2E

V7X_VMEM_BYTES = 64 * 1024 * 1024
VMEM_LIMIT_BYTES = V7X_VMEM_BYTES - 8 * 1024 * 1024
PROLOGUE_ROWS = 256
TM_RESIDENT = 256
WEIGHT_CHUNK_ROWS = 512
GLU_CHUNK_ROWS = 256
QKV_CHUNK_ROWS = 128
FOX_W_STAGES = 11
TM_SQUARE = 512


def _compiler_params(semantics):
    return pltpu.CompilerParams(dimension_semantics=semantics,
                                vmem_limit_bytes=VMEM_LIMIT_BYTES)


def _sigmoid(x):
    return 1.0 / (1.0 + jnp.exp(-x))


def _rms_rows(x, g):
    ms = jnp.mean(x * x, axis=-1, keepdims=True)
    return x * lax.rsqrt(ms + RMS_EPS) * g


def _prefixed_spec(prefix, block, index_map):
    return pl.BlockSpec((None,) * len(prefix) + block,
                        lambda *ids: prefix + index_map(*ids))


def _fused_matmul(a_list, row_consts, prologue, lhs_widths, terms, extras,
                  epilogue, out_dtype, n_out, tm, tn):
    m_rows = a_list[0][0].shape[-2]
    grid = (m_rows // tm, n_out // tn)
    n_a, n_c, n_t, n_e = len(a_list), len(row_consts), len(terms), len(extras)

    def body(*refs):
        a_refs = refs[:n_a]
        c_refs = refs[n_a:n_a + n_c]
        w_refs = refs[n_a + n_c:n_a + n_c + n_t]
        e_refs = refs[n_a + n_c + n_t:n_a + n_c + n_t + n_e]
        o_ref = refs[n_a + n_c + n_t + n_e]
        lhs_refs = refs[n_a + n_c + n_t + n_e + 1:]
        if prologue is not None:
            @pl.when(pl.program_id(1) == 0)
            def _():
                consts = [c[...] for c in c_refs]

                def rows_step(r, carry):
                    rows = pl.ds(pl.multiple_of(r * PROLOGUE_ROWS, PROLOGUE_ROWS),
                                 PROLOGUE_ROWS)
                    outs = prologue([a[rows, :] for a in a_refs], consts)
                    for dst, val in zip(lhs_refs, outs):
                        dst[rows, :] = val
                    return carry

                lax.fori_loop(0, tm // PROLOGUE_ROWS, rows_step, 0)
        else:
            lhs_refs = a_refs
        accs = [jnp.dot(lhs_refs[li][...], w[...].astype(BF16),
                        preferred_element_type=F32)
                for (li, _, _, _), w in zip(terms, w_refs)]
        o_ref[...] = epilogue(accs, [e[...] for e in e_refs]).astype(out_dtype)

    in_specs = [_prefixed_spec(pre, (tm, a.shape[-1]), lambda i, j: (i, 0))
                for a, pre in a_list]
    in_specs += [pl.BlockSpec((1, c.shape[1]), lambda i, j: (0, 0)) for c in row_consts]
    for _, w, pre, off in terms:
        in_specs.append(_prefixed_spec(
            pre, (w.shape[-2], tn), functools.partial(lambda i, j, o: (0, j + o), o=off)))
    for arr, kind in extras:
        if kind == 'tile':
            in_specs.append(pl.BlockSpec((tm, tn), lambda i, j: (i, j)))
        else:
            in_specs.append(pl.BlockSpec((1, tn), lambda i, j: (0, j)))
    scratch = []
    if prologue is not None:
        scratch = [pltpu.VMEM((tm, k), BF16) for k in lhs_widths]
    return pl.pallas_call(
        body,
        grid=grid,
        in_specs=in_specs,
        out_specs=pl.BlockSpec((tm, tn), lambda i, j: (i, j)),
        out_shape=jax.ShapeDtypeStruct((m_rows, n_out), out_dtype),
        scratch_shapes=scratch,
        compiler_params=_compiler_params(("parallel", "arbitrary")),
    )(*[a for a, _ in a_list], *row_consts, *[w for _, w, _, _ in terms],
      *[e for e, _ in extras])


def _norm_prologue(a_tiles, consts):
    return [_rms_rows(a_tiles[0], consts[0]).astype(BF16)]


def _resident_matmul(a_list, row_consts, prologue, terms, epilogue, outs, tm, n_stage,
                     transposed=False):
    m_rows = a_list[0][0].shape[-2]
    n_a, n_c, n_t, n_o = len(a_list), len(row_consts), len(terms), len(outs)
    n_axis = -2 if transposed else -1
    padded = [-(-w.shape[n_axis] // LANES) * LANES for _, w, _ in terms]

    def scratch_shape(w, n_pad):
        return (n_pad, w.shape[-1]) if transposed else (w.shape[-2], n_pad)

    def body(*refs):
        a_refs = refs[:n_a]
        c_refs = refs[n_a:n_a + n_c]
        w_refs = refs[n_a + n_c:n_a + n_c + n_t]
        o_refs = refs[n_a + n_c + n_t:n_a + n_c + n_t + n_o]
        wb_refs = refs[n_a + n_c + n_t + n_o:]
        g = pl.program_id(0)

        @pl.when(g == 0)
        def _():
            for (_, w, _), wb_ref, n_pad in zip(terms, wb_refs, padded):
                if n_pad == w.shape[n_axis]:
                    continue
                if transposed:
                    wb_ref[n_pad - LANES:, :] = jnp.zeros((LANES, w.shape[-1]), BF16)
                else:
                    wb_ref[:, n_pad - LANES:] = jnp.zeros((w.shape[-2], LANES), BF16)

        @pl.when(g < n_stage)
        def _():
            for (_, w, _), w_ref, wb_ref in zip(terms, w_refs, wb_refs):
                chunk = w.shape[-2] // n_stage
                rows = pl.ds(pl.multiple_of(g * chunk, chunk), chunk)
                wb_ref[rows, :w.shape[-1]] = w_ref[...].astype(BF16)

        def matmul(lhs, wb):
            if transposed:
                return _qk(lhs, wb)
            return jnp.dot(lhs, wb, preferred_element_type=F32)

        @pl.when(g >= n_stage)
        def _():
            a_tiles = [a[...] for a in a_refs]
            consts = [c[...] for c in c_refs]
            lhs = prologue(a_tiles, consts)
            accs = [matmul(lhs[li], wb[...]) for (li, _, _), wb in zip(terms, wb_refs)]
            for o_ref, tile, (_, dtype) in zip(o_refs, epilogue(accs, a_tiles, consts), outs):
                o_ref[...] = tile.astype(dtype)

    def row_tile(g):
        return (jnp.maximum(g - n_stage, 0), 0)

    in_specs = [_prefixed_spec(pre, (tm, a.shape[-1]), row_tile) for a, pre in a_list]
    in_specs += [pl.BlockSpec((1, c.shape[1]), lambda g: (0, 0)) for c in row_consts]
    for _, w, pre in terms:
        in_specs.append(_prefixed_spec(pre, (w.shape[-2] // n_stage, w.shape[-1]),
                                       lambda g: (jnp.minimum(g, n_stage - 1), 0)))
    results = pl.pallas_call(
        body,
        grid=(n_stage + m_rows // tm,),
        in_specs=in_specs,
        out_specs=[pl.BlockSpec((tm, n), row_tile) for n, _ in outs],
        out_shape=[jax.ShapeDtypeStruct((m_rows, n), dtype) for n, dtype in outs],
        scratch_shapes=[pltpu.VMEM(scratch_shape(w, n_pad), BF16)
                        for (_, w, _), n_pad in zip(terms, padded)],
        compiler_params=_compiler_params(("arbitrary",)),
    )(*[a for a, _ in a_list], *row_consts, *[w for _, w, _ in terms])
    return results


def _ffn(h, gain, w_in, w_out, tm, tn_in):
    d_model = h.shape[1]
    d_ff = w_out[0].shape[-2]

    def act_epilogue(accs, _):
        gate, up = accs
        return gate * _sigmoid(gate) * up

    act = _fused_matmul([(h, ())], [gain.reshape(1, d_model)], _norm_prologue, [d_model],
                        [(0, *w_in, 0), (0, *w_in, d_ff // tn_in)], [],
                        act_epilogue, BF16, d_ff, tm, tn_in)

    (out,) = _resident_matmul([(act, ()), (h, ())], [], lambda a_tiles, _: [a_tiles[0]],
                              [(0, *w_out)],
                              lambda accs, a_tiles, _: [a_tiles[1] + 0.5 * accs[0]],
                              [(d_model, F32)], TM_RESIDENT, d_ff // WEIGHT_CHUNK_ROWS)
    return out


def _ple(h, gain, p_i, w_gate, w_proj, tm, final_gain=None):
    d_model = h.shape[1]
    consts = [gain.reshape(1, d_model)]
    if final_gain is not None:
        consts.append(final_gain.reshape(1, d_model))

    def prologue(a_tiles, consts):
        return [_rms_rows(a_tiles[0], consts[0]).astype(BF16), a_tiles[1].astype(BF16)]

    def epilogue(accs, a_tiles, consts):
        out = a_tiles[0] + _sigmoid(accs[0]) * accs[1]
        return [out if final_gain is None else _rms_rows(out, consts[1])]

    (out,) = _resident_matmul([(h, ()), p_i], consts, prologue,
                              [(0, *w_gate), (1, *w_proj)], epilogue, [(d_model, F32)],
                              tm, d_model // WEIGHT_CHUNK_ROWS)
    return out


def _out_proj(a, w, h, tm):
    (out,) = _resident_matmul([(a, ()), (h, ())], [], lambda a_tiles, _: [a_tiles[0]],
                              [(0, *w)], lambda accs, a_tiles, _: [a_tiles[1] + accs[0]],
                              [(h.shape[1], F32)], tm, h.shape[1] // WEIGHT_CHUNK_ROWS)
    return out


def _norm_proj_resident(h, gain, w, tm):
    d_model = h.shape[1]
    (out,) = _resident_matmul([(h, ())], [gain.reshape(1, d_model)], _norm_prologue,
                              [(0, *w)], lambda accs, _a, _c: [accs[0]],
                              [(w[0].shape[-1], F32)], tm, d_model // WEIGHT_CHUNK_ROWS)
    return out


def _qkv_proj(h, gain, w, tm, n_stage, forget_bias=None, transposed=False):
    d_model = h.shape[1]
    consts = [gain.reshape(1, d_model)]
    outs = [(3 * d_model, BF16)]
    if forget_bias is not None:
        consts.append(forget_bias)
        outs.append((LANES, F32))

    def epilogue(accs, _a, consts):
        acc = accs[0]
        qkv = jnp.concatenate([acc[:, :d_model] * Q_PRESCALE, acc[:, d_model:3 * d_model]],
                              axis=1)
        if forget_bias is None:
            return [qkv]
        z = acc[:, 3 * d_model:] + consts[1]
        return [qkv, jnp.minimum(z, 0.0) - jnp.log(1.0 + jnp.exp(-jnp.abs(z)))]

    return _resident_matmul([(h, ())], consts, _norm_prologue, [(0, *w)], epilogue, outs,
                            tm, n_stage, transposed=transposed)


def _qk(q, k):
    return lax.dot_general(q, k, (((1,), (1,)), ((), ())), preferred_element_type=F32)


def _bf16_pieces(x):
    hi = x.astype(BF16).astype(F32)
    rest = x - hi
    mid = rest.astype(BF16).astype(F32)
    return hi, mid, rest - mid


def _lane_table(lane, entries, default=0.0):
    out = default
    for idx, val in reversed(list(entries.items())):
        out = jnp.where(lane == idx, val, out)
    return out


def _attend_all_tiles(q_ref, qaug_ref, kaug_ref, v_ref, o_ref, n_blk):
    blk = ATTN_BLOCK

    def logits(qi):
        rows = slice(qi * blk, (qi + 1) * blk)
        q_full = jnp.concatenate([q_ref[0, rows, :], qaug_ref[rows, :]], axis=1)
        return _qk(q_full, kaug_ref[0:(qi + 1) * blk, :])

    s = logits(0)
    for qi in range(n_blk):
        s_next = logits(qi + 1) if qi + 1 < n_blk else None
        o_ref[0, qi * blk:(qi + 1) * blk, :] = _softmax_pv(s, v_ref, qi + 1).astype(BF16)
        s = s_next


def _softmax_pv(s, v_ref, n_blocks):
    blk = ATTN_BLOCK
    n_keys = n_blocks * blk
    row = lax.broadcasted_iota(jnp.int32, (blk, blk), 0)
    col = lax.broadcasted_iota(jnp.int32, (blk, blk), 1)
    diag = jnp.where(row >= col, s[:, n_keys - blk:], MASK_VALUE)
    parts = [s[:, i * LANES:(i + 1) * LANES] for i in range((n_keys - blk) // LANES)]
    parts += [diag[:, :LANES], diag[:, LANES:]]
    m = jnp.max(functools.reduce(jnp.maximum, parts), axis=1, keepdims=True)
    probs = [jnp.exp2(part - m) for part in parts]
    denom = jnp.sum(functools.reduce(jnp.add, probs), axis=1, keepdims=True)
    p = jnp.concatenate([pr.astype(BF16) for pr in probs], axis=1)
    acc = jnp.dot(p, v_ref[0, 0:n_keys, :], preferred_element_type=F32)
    return acc / denom


def _log_forget_cumsum(logf):
    n_b, seq, lanes = logf.shape

    def body(x_ref, o_ref):
        x = x_ref[0]
        row = lax.broadcasted_iota(jnp.int32, (seq, lanes), 0)
        shift = 1
        while shift < seq:
            x = x + jnp.where(row >= shift, pltpu.roll(x, shift, 0), 0.0)
            shift *= 2
        o_ref[0] = x

    return pl.pallas_call(
        body,
        grid=(n_b,),
        in_specs=[pl.BlockSpec((1, seq, lanes), lambda b: (b, 0, 0))],
        out_specs=pl.BlockSpec((1, seq, lanes), lambda b: (b, 0, 0)),
        out_shape=jax.ShapeDtypeStruct((n_b, seq, lanes), F32),
        compiler_params=_compiler_params(("parallel",)),
    )(logf)


def _fox_attention(qkv, c):
    n_b, seq, _ = qkv.shape
    blk = ATTN_BLOCK
    n_blk = seq // blk

    def body(q_ref, k_ref, v_ref, c_ref, o_ref, kaug_ref, qaug_ref):
        h = pl.program_id(1)

        for r in range(n_blk):
            rows = slice(r * blk, (r + 1) * blk)
            lane = lax.broadcasted_iota(jnp.int32, (blk, LANES), 1)
            c_head = jnp.sum(jnp.where(lane == h, c_ref[0, rows, :], 0.0),
                             axis=1, keepdims=True) * LOG2E
            hi, mid, lo = _bf16_pieces(c_head)
            q_aug = _lane_table(lane, {0: hi, 1: mid, 2: lo, 3: 1.0, 4: 1.0, 5: 1.0})
            k_aug = _lane_table(lane, {0: 1.0, 1: 1.0, 2: 1.0, 3: -hi, 4: -mid, 5: -lo})
            qaug_ref[rows, :] = q_aug.astype(BF16)
            kaug_ref[rows, :LANES] = k_ref[0, rows, :]
            kaug_ref[rows, LANES:] = k_aug.astype(BF16)
        _attend_all_tiles(q_ref, qaug_ref, kaug_ref, v_ref, o_ref, n_blk)

    return pl.pallas_call(
        body,
        grid=(n_b, N_HEADS),
        in_specs=[
            pl.BlockSpec((1, seq, HEAD_DIM), lambda b, h: (b, 0, h)),
            pl.BlockSpec((1, seq, HEAD_DIM), lambda b, h: (b, 0, N_HEADS + h)),
            pl.BlockSpec((1, seq, HEAD_DIM), lambda b, h: (b, 0, 2 * N_HEADS + h)),
            pl.BlockSpec((1, seq, LANES), lambda b, h: (b, 0, 0)),
        ],
        out_specs=pl.BlockSpec((1, seq, HEAD_DIM), lambda b, h: (b, 0, h)),
        out_shape=jax.ShapeDtypeStruct((n_b, seq, N_HEADS * HEAD_DIM), BF16),
        scratch_shapes=[pltpu.VMEM((seq, 2 * LANES), BF16),
                        pltpu.VMEM((seq, LANES), BF16)],
        compiler_params=_compiler_params(("parallel", "arbitrary")),
    )(qkv, qkv, qkv, c)


def _fox_layer(h, gain, w_in, b_f, w_o, n_b, seq):
    d_model = h.shape[1]
    b_row = jnp.pad(b_f, (0, LANES - N_HEADS)).reshape(1, LANES)
    w_t = (jnp.swapaxes(w_in[0], -1, -2), w_in[1])
    qkv, logf = _qkv_proj(h, gain, w_t, TM_RESIDENT, FOX_W_STAGES, forget_bias=b_row,
                          transposed=True)
    c = _log_forget_cumsum(logf.reshape(n_b, seq, LANES))
    attn = _fox_attention(qkv.reshape(n_b, seq, 3 * d_model), c)
    return _out_proj(attn.reshape(n_b * seq, d_model), w_o, h, TM_SQUARE)


def _moba_attention(qkv, slopes):
    n_b, seq, _ = qkv.shape
    blk = MOBA_BLOCK
    assert blk == ATTN_BLOCK
    n_blk = seq // blk
    assert n_blk <= SUBLANES
    k_lanes = [SUBLANES, SUBLANES + 1, SUBLANES + 2]
    q_lanes = [SUBLANES + 3, SUBLANES + 4, SUBLANES + 5]

    def body(slopes_ref, q_ref, k_ref, v_ref, o_ref, kaug_ref, qaug_ref, kmean_ref):
        h = pl.program_id(1)
        slope2 = slopes_ref[h] * LOG2E

        kmean_ref[...] = jnp.zeros_like(kmean_ref)
        for n in range(n_blk):
            rows = slice(n * blk, (n + 1) * blk)
            kb = k_ref[0, rows, :]
            kmean_ref[n:n + 1, :] = jnp.mean(kb.astype(F32), axis=0, keepdims=True)
            lane = lax.broadcasted_iota(jnp.int32, (blk, LANES), 1)
            pos = (lax.broadcasted_iota(jnp.int32, (blk, 1), 0) + n * blk).astype(F32)
            hi, mid, lo = _bf16_pieces(slope2 * pos)
            entries = {n: 1.0, k_lanes[0]: hi, k_lanes[1]: mid, k_lanes[2]: lo,
                       q_lanes[0]: 1.0, q_lanes[1]: 1.0, q_lanes[2]: 1.0}
            kaug_ref[rows, :LANES] = kb
            kaug_ref[rows, LANES:] = _lane_table(lane, entries).astype(BF16)

        gate = _qk(kmean_ref[...].astype(BF16), q_ref[0])[:SUBLANES]
        key_blk = lax.broadcasted_iota(jnp.int32, (SUBLANES, seq), 0)
        q_pos = lax.broadcasted_iota(jnp.int32, (SUBLANES, seq), 1)
        q_blk = q_pos // blk
        beaten = jnp.zeros((SUBLANES, seq), F32)
        for n2 in range(n_blk):
            rival = gate[n2:n2 + 1, :]
            wins = jnp.where(key_blk > n2, jnp.where(rival >= gate, 1.0, 0.0),
                             jnp.where(rival > gate, 1.0, 0.0))
            beaten = beaten + jnp.where(n2 < q_blk, wins, 0.0)
        bias = jnp.where(key_blk < q_blk,
                         jnp.where(beaten < MOBA_TOP_K, 0.0, MASK_VALUE),
                         jnp.where(key_blk == q_blk, 0.0, MASK_VALUE))
        hi, mid, lo = _bf16_pieces(-slope2 * q_pos.astype(F32))
        pieces = jnp.where(key_blk < 3, 1.0,
                           jnp.where(key_blk == 3, hi,
                                     jnp.where(key_blk == 4, mid,
                                               jnp.where(key_blk == 5, lo, 0.0))))
        aug_t = jnp.concatenate(
            [bias, pieces, jnp.zeros((LANES - 2 * SUBLANES, seq), F32)], axis=0)
        for qi in range(n_blk):
            rows = slice(qi * blk, (qi + 1) * blk)
            qaug_ref[rows, :] = aug_t[:, rows].T.astype(BF16)
        _attend_all_tiles(q_ref, qaug_ref, kaug_ref, v_ref, o_ref, n_blk)

    return pl.pallas_call(
        body,
        grid=(n_b, N_HEADS),
        in_specs=[
            pl.BlockSpec(memory_space=pltpu.SMEM),
            pl.BlockSpec((1, seq, HEAD_DIM), lambda b, h: (b, 0, h)),
            pl.BlockSpec((1, seq, HEAD_DIM), lambda b, h: (b, 0, N_HEADS + h)),
            pl.BlockSpec((1, seq, HEAD_DIM), lambda b, h: (b, 0, 2 * N_HEADS + h)),
        ],
        out_specs=pl.BlockSpec((1, seq, HEAD_DIM), lambda b, h: (b, 0, h)),
        out_shape=jax.ShapeDtypeStruct((n_b, seq, N_HEADS * HEAD_DIM), BF16),
        scratch_shapes=[pltpu.VMEM((seq, 2 * LANES), BF16),
                        pltpu.VMEM((seq, LANES), BF16),
                        pltpu.VMEM((2 * SUBLANES, HEAD_DIM), F32)],
        compiler_params=_compiler_params(("parallel", "arbitrary")),
    )(slopes, qkv, qkv, qkv)


def _moba_layer(h, gain, w_qkv, w_o, n_b, seq):
    d_model = h.shape[1]
    (qkv,) = _qkv_proj(h, gain, w_qkv, TM_RESIDENT, d_model // QKV_CHUNK_ROWS)
    slopes = 2.0 ** (-8.0 * jnp.arange(1, N_HEADS + 1, dtype=F32) / N_HEADS)
    attn = _moba_attention(qkv.reshape(n_b, seq, 3 * d_model), slopes)
    return _out_proj(attn.reshape(n_b * seq, d_model), w_o, h, TM_SQUARE)


S5_BLOCK_GROUPS = LANES // S5_GROUP
S5_PAIR_ROWS = 2 * S5_GROUP
S5_PAIRS = S5_BLOCK_GROUPS // 2


def _s5_pair_tiles(w, n_lb):
    n_g, n_k, n_c, n_p = w.shape
    w = w.reshape(n_lb, S5_PAIRS, 2, n_k, n_c, n_p)
    even = jnp.pad(w[:, :, 0], ((0, 0),) * 4 + ((0, n_p),))
    odd = jnp.pad(w[:, :, 1], ((0, 0),) * 4 + ((n_p, 0),))
    return jnp.concatenate([even, odd], axis=3).astype(BF16)


def _s5_weights(a_re, a_im, log_dt, b_re, b_im, c_re, c_im, n_levels):
    n_g, n_p = a_re.shape
    n_lb = n_g // S5_BLOCK_GROUPS
    t_len = S5_CHUNK
    a = lax.complex(a_re.astype(F32), a_im.astype(F32))
    dt = jnp.exp(log_dt.astype(F32))[:, None]
    a_dt = a * dt
    a_bar = jnp.exp(a_dt)
    b_bar = ((a_bar - 1.0) / a)[..., None] * lax.complex(b_re.astype(F32), b_im.astype(F32))
    c_mat = lax.complex(c_re.astype(F32), c_im.astype(F32))
    steps = jnp.arange(t_len + 1, dtype=F32)
    a_pow = jnp.exp(a_dt[:, None, :] * steps[None, :, None])

    inp_c = a_pow[:, t_len - 1::-1][:, :t_len, None, :] * b_bar.transpose(0, 2, 1)[:, None]
    out_c = c_mat[:, None] * a_pow[:, :, None, :]
    w_in = (_s5_pair_tiles(inp_c.real, n_lb), _s5_pair_tiles(inp_c.imag, n_lb))
    w_out = (_s5_pair_tiles(out_c.real, n_lb), _s5_pair_tiles(-out_c.imag, n_lb))

    jumps = (t_len * 2 ** jnp.arange(n_levels)).astype(F32)
    d = jnp.exp(a_dt[:, None, :] * jumps[None, :, None])
    d = d.reshape(n_lb, S5_BLOCK_GROUPS, n_levels, n_p).transpose(0, 2, 1, 3)
    d = d.reshape(n_lb, n_levels, S5_BLOCK_GROUPS * n_p)
    decay = jnp.stack([d.real, d.imag], axis=2)
    return w_in, w_out, decay


def _s5_scan(u, w_in, w_out, decay, d_skip, n_chunks):
    t_len, n_rows, d_model = u.shape
    n_levels = decay.shape[1]
    half = decay.shape[3]
    wide = t_len * LANES

    def body(u_ref, wre_ref, wim_ref, vre_ref, vim_ref, dec_ref, d_ref,
             z_ref, a_scr, conv_scr, win_scr, vpow_scr):
        @pl.when(pl.program_id(0) == 0)
        def _():
            conv_scr[...] = jnp.zeros_like(conv_scr)
            win_scr[...] = jnp.zeros_like(win_scr)
            vpow_scr[...] = jnp.zeros_like(vpow_scr)

        def place(dst, k, re_ref, im_ref):
            for q in range(S5_PAIRS):
                rows = slice(k * LANES + q * S5_PAIR_ROWS, k * LANES + (q + 1) * S5_PAIR_ROWS)
                dst[rows, q * LANES:(q + 1) * LANES] = re_ref[0, q, k]
                dst[rows, half + q * LANES:half + (q + 1) * LANES] = im_ref[0, q, k]

        for t in range(t_len):
            a_scr[:, t * LANES:(t + 1) * LANES] = u_ref[t].astype(BF16)
            place(win_scr, t, wre_ref, wim_ref)
        for k in range(t_len + 1):
            place(vpow_scr, k, vre_ref, vim_ref)
        taps = _qk(win_scr[(t_len - 1) * LANES:, :], vpow_scr[:wide, :]).astype(BF16)
        for t in range(t_len):
            for t2 in range(t, t_len):
                conv_scr[t * LANES:(t + 1) * LANES, t2 * LANES:(t2 + 1) * LANES] = (
                    taps[:, (t2 - t) * LANES:(t2 - t + 1) * LANES])

        a = a_scr[...]
        x = jnp.dot(a, win_scr[...], preferred_element_type=F32)
        h_re, h_im = x[:, :half], x[:, half:]
        chunk = lax.broadcasted_iota(jnp.int32, (n_rows, half), 0) % n_chunks

        def shifted(v, n):
            return jnp.where(chunk >= n, pltpu.roll(v, n, 0), 0.0)

        for lvl in range(n_levels):
            p_re, p_im = shifted(h_re, 2 ** lvl), shifted(h_im, 2 ** lvl)
            d_re, d_im = dec_ref[0, lvl, 0:1, :], dec_ref[0, lvl, 1:2, :]
            h_re, h_im = (h_re + d_re * p_re - d_im * p_im,
                          h_im + d_re * p_im + d_im * p_re)
        h_in = jnp.concatenate([shifted(h_re, 1), shifted(h_im, 1)], axis=1).astype(BF16)
        y_state = _qk(h_in, vpow_scr[LANES:, :])
        n_split = 4
        per = t_len // n_split
        for part in range(n_split):
            k_hi = (part + 1) * per * LANES
            cols = slice(part * per * LANES, k_hi)
            y = y_state[:, cols] + jnp.dot(a_scr[:, :k_hi], conv_scr[:k_hi, cols],
                                           preferred_element_type=F32)
            for t in range(part * per, (part + 1) * per):
                lo = (t - part * per) * LANES
                z = y[:, lo:lo + LANES] + d_ref[...] * u_ref[t]
                z_ref[t] = jax.nn.gelu(z).astype(BF16)

    def pair_spec(n_k):
        return pl.BlockSpec((1, S5_PAIRS, n_k, S5_PAIR_ROWS, LANES),
                            lambda i: (i, 0, 0, 0, 0))

    return pl.pallas_call(
        body,
        grid=(d_model // LANES,),
        in_specs=[
            pl.BlockSpec((t_len, n_rows, LANES), lambda i: (0, 0, i)),
            pair_spec(t_len), pair_spec(t_len), pair_spec(t_len + 1), pair_spec(t_len + 1),
            pl.BlockSpec((1, n_levels, 2, half), lambda i: (i, 0, 0, 0)),
            pl.BlockSpec((1, LANES), lambda i: (0, i)),
        ],
        out_specs=pl.BlockSpec((t_len, n_rows, LANES), lambda i: (0, 0, i)),
        out_shape=jax.ShapeDtypeStruct((t_len, n_rows, d_model), BF16),
        scratch_shapes=[pltpu.VMEM((n_rows, wide), BF16),
                        pltpu.VMEM((wide, wide), BF16),
                        pltpu.VMEM((wide, 2 * half), BF16),
                        pltpu.VMEM((wide + LANES, 2 * half), BF16)],
        compiler_params=_compiler_params(("arbitrary",)),
    )(u, *w_in, *w_out, decay, d_skip)


def _s5_layer(h, gain, w_in, a_re, a_im, log_dt, b_re, b_im, c_re, c_im, d_skip,
              w_glu, n_b, seq):
    m_rows, d_model = h.shape
    t_len = S5_CHUNK
    n_chunks = seq // t_len
    n_levels = max(1, math.ceil(math.log2(n_chunks)))
    hp = h.reshape(n_b, n_chunks, t_len, d_model).transpose(2, 0, 1, 3).reshape(m_rows, d_model)
    u = _norm_proj_resident(hp, gain, w_in, TM_SQUARE)
    s5_in, s5_out, decay = _s5_weights(a_re, a_im, log_dt, b_re, b_im, c_re, c_im, n_levels)
    z = _s5_scan(u.reshape(t_len, n_b * n_chunks, d_model), s5_in, s5_out, decay,
                 d_skip.astype(F32).reshape(1, d_model), n_chunks)

    def glu_epilogue(accs, a_tiles, _):
        return [a_tiles[1] + accs[0][:, :d_model] * _sigmoid(accs[0][:, d_model:])]

    (hp,) = _resident_matmul([(z.reshape(m_rows, d_model), ()), (hp, ())], [],
                             lambda a_tiles, _: [a_tiles[0]], [(0, *w_glu)], glu_epilogue,
                             [(d_model, F32)], TM_RESIDENT, d_model // GLU_CHUNK_ROWS)
    return hp.reshape(t_len, n_b, n_chunks, d_model).transpose(1, 2, 0, 3).reshape(m_rows, d_model)


@jax.jit
def _trunk(x, p, norm_g, final_g, w_ffn_in, w_ffn_out, w_ple_gate, w_ple_proj,
           fox_w_in, fox_b_f, fox_w_o, moba_w_qkv, moba_w_o,
           s5_w_in, s5_a_re, s5_a_im, s5_log_dt, s5_b_re, s5_b_im,
           s5_c_re, s5_c_im, s5_d, s5_w_glu):
    n_b, seq, d_model = x.shape
    depth = norm_g.shape[0]
    m_rows = n_b * seq
    tm = min(1024, m_rows)
    tn = 512
    h = x.reshape(m_rows, d_model)
    p_rows = p.reshape(depth, m_rows, p.shape[-1])
    for i in range(depth):
        g = norm_g[i]
        h = _ffn(h, g[0], (w_ffn_in, (i, 0)), (w_ffn_out, (i, 0)), tm, tn)
        kind, j = i % 3, i // 3
        if kind == 0:
            h = _fox_layer(h, g[1], (fox_w_in, (j,)), fox_b_f[j], (fox_w_o, (j,)), n_b, seq)
        elif kind == 1:
            h = _moba_layer(h, g[1], (moba_w_qkv, (j,)), (moba_w_o, (j,)), n_b, seq)
        else:
            h = _s5_layer(h, g[1], (s5_w_in, (j,)), s5_a_re[j], s5_a_im[j],
                          s5_log_dt[j], s5_b_re[j], s5_b_im[j], s5_c_re[j], s5_c_im[j],
                          s5_d[j], (s5_w_glu, (j,)), n_b, seq)
        h = _ffn(h, g[2], (w_ffn_in, (i, 1)), (w_ffn_out, (i, 1)), tm, tn)
        h = _ple(h, g[3], (p_rows, (i,)), (w_ple_gate, (i,)), (w_ple_proj, (i,)),
                 TM_SQUARE, final_gain=final_g if i == depth - 1 else None)
    return h.reshape(n_b, seq, d_model)


def kernel(x, p, norm_g, final_g, w_ffn_in, w_ffn_out, w_ple_gate, w_ple_proj, fox_w_in, fox_b_f, fox_w_o, moba_w_qkv, moba_w_o, s5_w_in, s5_a_re, s5_a_im, s5_log_dt, s5_b_re, s5_b_im, s5_c_re, s5_c_im, s5_d, s5_w_glu):
    return _trunk(x, p, norm_g, final_g, w_ffn_in, w_ffn_out, w_ple_gate, w_ple_proj,
                  fox_w_in, fox_b_f, fox_w_o, moba_w_qkv, moba_w_o,
                  s5_w_in, s5_a_re, s5_a_im, s5_log_dt, s5_b_re, s5_b_im,
                  s5_c_re, s5_c_im, s5_d, s5_w_glu)
```

```python
import functools
import math

import jax
import jax.numpy as jnp
from jax import lax
from jax.experimental import pallas as pl
from jax.experimental.pallas import tpu as pltpu

F32 = jnp.float32
BF16 = jnp.bfloat16

N_HEADS = 16
HEAD_DIM = 128
LANES = 128
SUBLANES = 8
ATTN_BLOCK = 256
MOBA_BLOCK = 256
MOBA_TOP_K = 3
S5_GROUP = 16
S5_STATE = 64
S5_CHUNK = 16
RMS_EPS = 1e-6
MASK_VALUE = -1e30
LOG2E = math.log2(math.e)
Q_PRESCALE = HEAD_DIM ** -0.5 * LOG2E

V7X_VMEM_BYTES = 64 * 1024 * 1024
VMEM_LIMIT_BYTES = V7X_VMEM_BYTES - 8 * 1024 * 1024
PROLOGUE_ROWS = 256
EPILOGUE_ROWS = 256
TM_RESIDENT = 256
WEIGHT_CHUNK_ROWS = 512
GLU_CHUNK_ROWS = 256
QKV_CHUNK_ROWS = 128
FOX_W_STAGES = 11
TM_SQUARE = 512
TM_FFN_NORMED = 2048


def _compiler_params(semantics):
    return pltpu.CompilerParams(dimension_semantics=semantics,
                                vmem_limit_bytes=VMEM_LIMIT_BYTES)


def _sigmoid(x):
    return 1.0 / (1.0 + jnp.exp(-x))


def _rms_rows(x, g):
    ms = jnp.mean(x * x, axis=-1, keepdims=True)
    return x * lax.rsqrt(ms + RMS_EPS) * g


def _prefixed_spec(prefix, block, index_map):
    return pl.BlockSpec((None,) * len(prefix) + block,
                        lambda *ids: prefix + index_map(*ids))


def _fused_matmul(a_list, row_consts, prologue, lhs_widths, terms, extras,
                  epilogue, out_dtype, n_out, tm, tn):
    m_rows = a_list[0][0].shape[-2]
    grid = (m_rows // tm, n_out // tn)
    n_a, n_c, n_t, n_e = len(a_list), len(row_consts), len(terms), len(extras)

    def body(*refs):
        a_refs = refs[:n_a]
        c_refs = refs[n_a:n_a + n_c]
        w_refs = refs[n_a + n_c:n_a + n_c + n_t]
        e_refs = refs[n_a + n_c + n_t:n_a + n_c + n_t + n_e]
        o_ref = refs[n_a + n_c + n_t + n_e]
        lhs_refs = refs[n_a + n_c + n_t + n_e + 1:]
        if prologue is not None:
            @pl.when(pl.program_id(1) == 0)
            def _():
                consts = [c[...] for c in c_refs]

                def rows_step(r, carry):
                    rows = pl.ds(pl.multiple_of(r * PROLOGUE_ROWS, PROLOGUE_ROWS),
                                 PROLOGUE_ROWS)
                    outs = prologue([a[rows, :] for a in a_refs], consts)
                    for dst, val in zip(lhs_refs, outs):
                        dst[rows, :] = val
                    return carry

                lax.fori_loop(0, tm // PROLOGUE_ROWS, rows_step, 0)
        else:
            lhs_refs = a_refs
        weights = [w[...].astype(BF16) for w in w_refs]
        for r0 in range(0, tm, EPILOGUE_ROWS):
            rows = slice(r0, r0 + EPILOGUE_ROWS)
            accs = [jnp.dot(lhs_refs[li][rows, :], w, preferred_element_type=F32)
                    for (li, _, _, _), w in zip(terms, weights)]
            extra_tiles = [e[rows, :] if kind == 'tile' else e[...]
                           for e, (_, kind) in zip(e_refs, extras)]
            o_ref[rows, :] = epilogue(accs, extra_tiles).astype(out_dtype)

    in_specs = [_prefixed_spec(pre, (tm, a.shape[-1]), lambda i, j: (i, 0))
                for a, pre in a_list]
    in_specs += [pl.BlockSpec((1, c.shape[1]), lambda i, j: (0, 0)) for c in row_consts]
    for _, w, pre, off in terms:
        in_specs.append(_prefixed_spec(
            pre, (w.shape[-2], tn), functools.partial(lambda i, j, o: (0, j + o), o=off)))
    for arr, kind in extras:
        if kind == 'tile':
            in_specs.append(pl.BlockSpec((tm, tn), lambda i, j: (i, j)))
        else:
            in_specs.append(pl.BlockSpec((1, tn), lambda i, j: (0, j)))
    scratch = []
    if prologue is not None:
        scratch = [pltpu.VMEM((tm, k), BF16) for k in lhs_widths]
    return pl.pallas_call(
        body,
        grid=grid,
        in_specs=in_specs,
        out_specs=pl.BlockSpec((tm, tn), lambda i, j: (i, j)),
        out_shape=jax.ShapeDtypeStruct((m_rows, n_out), out_dtype),
        scratch_shapes=scratch,
        compiler_params=_compiler_params(("parallel", "arbitrary")),
    )(*[a for a, _ in a_list], *row_consts, *[w for _, w, _, _ in terms],
      *[e for e, _ in extras])


def _norm_prologue(a_tiles, consts):
    return [_rms_rows(a_tiles[0], consts[0]).astype(BF16)]


def _resident_matmul(a_list, row_consts, prologue, terms, epilogue, outs, tm, n_stage,
                     transposed=False):
    m_rows = a_list[0][0].shape[-2]
    n_a, n_c, n_t, n_o = len(a_list), len(row_consts), len(terms), len(outs)
    n_axis = -2 if transposed else -1
    padded = [-(-w.shape[n_axis] // LANES) * LANES for _, w, _ in terms]

    def scratch_shape(w, n_pad):
        return (n_pad, w.shape[-1]) if transposed else (w.shape[-2], n_pad)

    def body(*refs):
        a_refs = refs[:n_a]
        c_refs = refs[n_a:n_a + n_c]
        w_refs = refs[n_a + n_c:n_a + n_c + n_t]
        o_refs = refs[n_a + n_c + n_t:n_a + n_c + n_t + n_o]
        wb_refs = refs[n_a + n_c + n_t + n_o:]
        g = pl.program_id(0)

        @pl.when(g == 0)
        def _():
            for (_, w, _), wb_ref, n_pad in zip(terms, wb_refs, padded):
                if n_pad == w.shape[n_axis]:
                    continue
                if transposed:
                    wb_ref[n_pad - LANES:, :] = jnp.zeros((LANES, w.shape[-1]), BF16)
                else:
                    wb_ref[:, n_pad - LANES:] = jnp.zeros((w.shape[-2], LANES), BF16)

        @pl.when(g < n_stage)
        def _():
            for (_, w, _), w_ref, wb_ref in zip(terms, w_refs, wb_refs):
                chunk = w.shape[-2] // n_stage
                rows = pl.ds(pl.multiple_of(g * chunk, chunk), chunk)
                wb_ref[rows, :w.shape[-1]] = w_ref[...].astype(BF16)

        def matmul(lhs, wb):
            if transposed:
                return _qk(lhs, wb)
            return jnp.dot(lhs, wb, preferred_element_type=F32)

        @pl.when(g >= n_stage)
        def _():
            a_tiles = [a[...] for a in a_refs]
            consts = [c[...] for c in c_refs]
            lhs = prologue(a_tiles, consts)
            accs = [matmul(lhs[li], wb[...]) for (li, _, _), wb in zip(terms, wb_refs)]
            for o_ref, tile, (_, dtype) in zip(o_refs, epilogue(accs, a_tiles, consts), outs):
                o_ref[...] = tile.astype(dtype)

    def row_tile(g):
        return (jnp.maximum(g - n_stage, 0), 0)

    in_specs = [_prefixed_spec(pre, (tm, a.shape[-1]), row_tile) for a, pre in a_list]
    in_specs += [pl.BlockSpec((1, c.shape[1]), lambda g: (0, 0)) for c in row_consts]
    for _, w, pre in terms:
        in_specs.append(_prefixed_spec(pre, (w.shape[-2] // n_stage, w.shape[-1]),
                                       lambda g: (jnp.minimum(g, n_stage - 1), 0)))
    results = pl.pallas_call(
        body,
        grid=(n_stage + m_rows // tm,),
        in_specs=in_specs,
        out_specs=[pl.BlockSpec((tm, n), row_tile) for n, _ in outs],
        out_shape=[jax.ShapeDtypeStruct((m_rows, n), dtype) for n, dtype in outs],
        scratch_shapes=[pltpu.VMEM(scratch_shape(w, n_pad), BF16)
                        for (_, w, _), n_pad in zip(terms, padded)],
        compiler_params=_compiler_params(("arbitrary",)),
    )(*[a for a, _ in a_list], *row_consts, *[w for _, w, _ in terms])
    return results


def _ffn(h, gain, w_in, w_out, tm, tn_in, xn=None):
    d_model = h.shape[1]
    d_ff = w_out[0].shape[-2]
    terms = [(0, *w_in, 0), (0, *w_in, d_ff // tn_in)]

    def act_epilogue(accs, _):
        gate, up = accs
        return gate * _sigmoid(gate) * up

    if xn is None:
        act = _fused_matmul([(h, ())], [gain.reshape(1, d_model)], _norm_prologue,
                            [d_model], terms, [], act_epilogue, BF16, d_ff, tm, tn_in)
    else:
        act = _fused_matmul([(xn, ())], [], None, [], terms, [], act_epilogue, BF16, d_ff,
                            min(TM_FFN_NORMED, h.shape[0]), tn_in)

    (out,) = _resident_matmul([(act, ()), (h, ())], [], lambda a_tiles, _: [a_tiles[0]],
                              [(0, *w_out)],
                              lambda accs, a_tiles, _: [a_tiles[1] + 0.5 * accs[0]],
                              [(d_model, F32)], TM_RESIDENT, d_ff // WEIGHT_CHUNK_ROWS)
    return out


def _ple(h, gain, p_i, w_gate, w_proj, tm, final_gain=None, next_gain=None):
    d_model = h.shape[1]
    consts = [gain.reshape(1, d_model)]
    outs = [(d_model, F32)]
    if final_gain is not None:
        consts.append(final_gain.reshape(1, d_model))
    if next_gain is not None:
        consts.append(next_gain.reshape(1, d_model))
        outs.append((d_model, BF16))

    def prologue(a_tiles, consts):
        return [_rms_rows(a_tiles[0], consts[0]).astype(BF16), a_tiles[1].astype(BF16)]

    def epilogue(accs, a_tiles, consts):
        out = a_tiles[0] + _sigmoid(accs[0]) * accs[1]
        if final_gain is not None:
            return [_rms_rows(out, consts[1])]
        if next_gain is not None:
            return [out, _rms_rows(out, consts[1])]
        return [out]

    res = _resident_matmul([(h, ()), p_i], consts, prologue, [(0, *w_gate), (1, *w_proj)],
                           epilogue, outs, tm, d_model // WEIGHT_CHUNK_ROWS)
    return res[0], (res[1] if next_gain is not None else None)


def _out_proj(a, w, h, tm, next_gain):
    d_model = h.shape[1]

    def epilogue(accs, a_tiles, consts):
        out = a_tiles[1] + accs[0]
        return [out, _rms_rows(out, consts[0])]

    return _resident_matmul([(a, ()), (h, ())], [next_gain.reshape(1, d_model)],
                            lambda a_tiles, _: [a_tiles[0]], [(0, *w)], epilogue,
                            [(d_model, F32), (d_model, BF16)], tm,
                            d_model // WEIGHT_CHUNK_ROWS)


def _norm_proj_resident(h, gain, w, tm):
    d_model = h.shape[1]
    (out,) = _resident_matmul([(h, ())], [gain.reshape(1, d_model)], _norm_prologue,
                              [(0, *w)], lambda accs, _a, _c: [accs[0]],
                              [(w[0].shape[-1], F32)], tm, d_model // WEIGHT_CHUNK_ROWS)
    return out


def _qkv_proj(h, gain, w, tm, n_stage, forget_bias=None, transposed=False):
    d_model = h.shape[1]
    consts = [gain.reshape(1, d_model)]
    outs = [(3 * d_model, BF16)]
    if forget_bias is not None:
        consts.append(forget_bias)
        outs.append((LANES, F32))

    def epilogue(accs, _a, consts):
        acc = accs[0]
        qkv = jnp.concatenate([acc[:, :d_model] * Q_PRESCALE, acc[:, d_model:3 * d_model]],
                              axis=1)
        if forget_bias is None:
            return [qkv]
        z = acc[:, 3 * d_model:] + consts[1]
        return [qkv, jnp.minimum(z, 0.0) - jnp.log(1.0 + jnp.exp(-jnp.abs(z)))]

    return _resident_matmul([(h, ())], consts, _norm_prologue, [(0, *w)], epilogue, outs,
                            tm, n_stage, transposed=transposed)


def _qk(q, k):
    return lax.dot_general(q, k, (((1,), (1,)), ((), ())), preferred_element_type=F32)


def _bf16_pieces(x):
    hi = x.astype(BF16).astype(F32)
    rest = x - hi
    mid = rest.astype(BF16).astype(F32)
    return hi, mid, rest - mid


def _lane_table(lane, entries, default=0.0):
    out = default
    for idx, val in reversed(list(entries.items())):
        out = jnp.where(lane == idx, val, out)
    return out


def _attend_all_tiles(q_ref, qaug_ref, kaug_ref, v_ref, o_ref, n_blk):
    blk = ATTN_BLOCK

    def logits(qi):
        rows = slice(qi * blk, (qi + 1) * blk)
        q_full = jnp.concatenate([q_ref[0, rows, :], qaug_ref[rows, :]], axis=1)
        return _qk(q_full, kaug_ref[0:(qi + 1) * blk, :])

    s = logits(0)
    for qi in range(n_blk):
        s_next = logits(qi + 1) if qi + 1 < n_blk else None
        o_ref[0, qi * blk:(qi + 1) * blk, :] = _softmax_pv(s, v_ref, qi + 1).astype(BF16)
        s = s_next


def _softmax_pv(s, v_ref, n_blocks):
    blk = ATTN_BLOCK
    n_keys = n_blocks * blk
    row = lax.broadcasted_iota(jnp.int32, (blk, blk), 0)
    col = lax.broadcasted_iota(jnp.int32, (blk, blk), 1)
    diag = jnp.where(row >= col, s[:, n_keys - blk:], MASK_VALUE)
    parts = [s[:, i * LANES:(i + 1) * LANES] for i in range((n_keys - blk) // LANES)]
    parts += [diag[:, :LANES], diag[:, LANES:]]
    m = jnp.max(functools.reduce(jnp.maximum, parts), axis=1, keepdims=True)
    probs = [jnp.exp2(part - m) for part in parts]
    denom = jnp.sum(functools.reduce(jnp.add, probs), axis=1, keepdims=True)
    p = jnp.concatenate([pr.astype(BF16) for pr in probs], axis=1)
    acc = jnp.dot(p, v_ref[0, 0:n_keys, :], preferred_element_type=F32)
    return acc / denom


def _log_forget_cumsum(logf):
    n_b, seq, lanes = logf.shape

    def body(x_ref, o_ref):
        x = x_ref[0]
        row = lax.broadcasted_iota(jnp.int32, (seq, lanes), 0)
        shift = 1
        while shift < seq:
            x = x + jnp.where(row >= shift, pltpu.roll(x, shift, 0), 0.0)
            shift *= 2
        o_ref[0] = x

    return pl.pallas_call(
        body,
        grid=(n_b,),
        in_specs=[pl.BlockSpec((1, seq, lanes), lambda b: (b, 0, 0))],
        out_specs=pl.BlockSpec((1, seq, lanes), lambda b: (b, 0, 0)),
        out_shape=jax.ShapeDtypeStruct((n_b, seq, lanes), F32),
        compiler_params=_compiler_params(("parallel",)),
    )(logf)


def _fox_attention(qkv, c):
    n_b, seq, _ = qkv.shape
    blk = ATTN_BLOCK
    n_blk = seq // blk

    def body(q_ref, k_ref, v_ref, c_ref, o_ref, kaug_ref, qaug_ref):
        h = pl.program_id(1)

        for r in range(n_blk):
            rows = slice(r * blk, (r + 1) * blk)
            lane = lax.broadcasted_iota(jnp.int32, (blk, LANES), 1)
            c_head = jnp.sum(jnp.where(lane == h, c_ref[0, rows, :], 0.0),
                             axis=1, keepdims=True) * LOG2E
            hi, mid, lo = _bf16_pieces(c_head)
            q_aug = _lane_table(lane, {0: hi, 1: mid, 2: lo, 3: 1.0, 4: 1.0, 5: 1.0})
            k_aug = _lane_table(lane, {0: 1.0, 1: 1.0, 2: 1.0, 3: -hi, 4: -mid, 5: -lo})
            qaug_ref[rows, :] = q_aug.astype(BF16)
            kaug_ref[rows, :LANES] = k_ref[0, rows, :]
            kaug_ref[rows, LANES:] = k_aug.astype(BF16)
        _attend_all_tiles(q_ref, qaug_ref, kaug_ref, v_ref, o_ref, n_blk)

    return pl.pallas_call(
        body,
        grid=(n_b, N_HEADS),
        in_specs=[
            pl.BlockSpec((1, seq, HEAD_DIM), lambda b, h: (b, 0, h)),
            pl.BlockSpec((1, seq, HEAD_DIM), lambda b, h: (b, 0, N_HEADS + h)),
            pl.BlockSpec((1, seq, HEAD_DIM), lambda b, h: (b, 0, 2 * N_HEADS + h)),
            pl.BlockSpec((1, seq, LANES), lambda b, h: (b, 0, 0)),
        ],
        out_specs=pl.BlockSpec((1, seq, HEAD_DIM), lambda b, h: (b, 0, h)),
        out_shape=jax.ShapeDtypeStruct((n_b, seq, N_HEADS * HEAD_DIM), BF16),
        scratch_shapes=[pltpu.VMEM((seq, 2 * LANES), BF16),
                        pltpu.VMEM((seq, LANES), BF16)],
        compiler_params=_compiler_params(("parallel", "arbitrary")),
    )(qkv, qkv, qkv, c)


def _fox_layer(h, gain, w_in, b_f, w_o, n_b, seq, next_gain):
    d_model = h.shape[1]
    b_row = jnp.pad(b_f, (0, LANES - N_HEADS)).reshape(1, LANES)
    w_t = (jnp.swapaxes(w_in[0], -1, -2), w_in[1])
    qkv, logf = _qkv_proj(h, gain, w_t, TM_RESIDENT, FOX_W_STAGES, forget_bias=b_row,
                          transposed=True)
    c = _log_forget_cumsum(logf.reshape(n_b, seq, LANES))
    attn = _fox_attention(qkv.reshape(n_b, seq, 3 * d_model), c)
    return _out_proj(attn.reshape(n_b * seq, d_model), w_o, h, TM_SQUARE, next_gain)


def _moba_attention(qkv, slopes):
    n_b, seq, _ = qkv.shape
    blk = MOBA_BLOCK
    assert blk == ATTN_BLOCK
    n_blk = seq // blk
    assert n_blk <= SUBLANES
    k_lanes = [SUBLANES, SUBLANES + 1, SUBLANES + 2]
    q_lanes = [SUBLANES + 3, SUBLANES + 4, SUBLANES + 5]

    def body(slopes_ref, q_ref, k_ref, v_ref, o_ref, kaug_ref, qaug_ref, kmean_ref):
        h = pl.program_id(1)
        slope2 = slopes_ref[h] * LOG2E

        kmean_ref[...] = jnp.zeros_like(kmean_ref)
        for n in range(n_blk):
            rows = slice(n * blk, (n + 1) * blk)
            kb = k_ref[0, rows, :]
            kmean_ref[n:n + 1, :] = jnp.mean(kb.astype(F32), axis=0, keepdims=True)
            lane = lax.broadcasted_iota(jnp.int32, (blk, LANES), 1)
            pos = (lax.broadcasted_iota(jnp.int32, (blk, 1), 0) + n * blk).astype(F32)
            hi, mid, lo = _bf16_pieces(slope2 * pos)
            entries = {n: 1.0, k_lanes[0]: hi, k_lanes[1]: mid, k_lanes[2]: lo,
                       q_lanes[0]: 1.0, q_lanes[1]: 1.0, q_lanes[2]: 1.0}
            kaug_ref[rows, :LANES] = kb
            kaug_ref[rows, LANES:] = _lane_table(lane, entries).astype(BF16)

        gate = _qk(kmean_ref[...].astype(BF16), q_ref[0])[:SUBLANES]
        key_blk = lax.broadcasted_iota(jnp.int32, (SUBLANES, seq), 0)
        q_pos = lax.broadcasted_iota(jnp.int32, (SUBLANES, seq), 1)
        q_blk = q_pos // blk
        beaten = jnp.zeros((SUBLANES, seq), F32)
        for n2 in range(n_blk):
            rival = gate[n2:n2 + 1, :]
            wins = jnp.where(key_blk > n2, jnp.where(rival >= gate, 1.0, 0.0),
                             jnp.where(rival > gate, 1.0, 0.0))
            beaten = beaten + jnp.where(n2 < q_blk, wins, 0.0)
        bias = jnp.where(key_blk < q_blk,
                         jnp.where(beaten < MOBA_TOP_K, 0.0, MASK_VALUE),
                         jnp.where(key_blk == q_blk, 0.0, MASK_VALUE))
        hi, mid, lo = _bf16_pieces(-slope2 * q_pos.astype(F32))
        pieces = jnp.where(key_blk < 3, 1.0,
                           jnp.where(key_blk == 3, hi,
                                     jnp.where(key_blk == 4, mid,
                                               jnp.where(key_blk == 5, lo, 0.0))))
        aug_t = jnp.concatenate(
            [bias, pieces, jnp.zeros((LANES - 2 * SUBLANES, seq), F32)], axis=0)
        for qi in range(n_blk):
            rows = slice(qi * blk, (qi + 1) * blk)
            qaug_ref[rows, :] = aug_t[:, rows].T.astype(BF16)
        _attend_all_tiles(q_ref, qaug_ref, kaug_ref, v_ref, o_ref, n_blk)

    return pl.pallas_call(
        body,
        grid=(n_b, N_HEADS),
        in_specs=[
            pl.BlockSpec(memory_space=pltpu.SMEM),
            pl.BlockSpec((1, seq, HEAD_DIM), lambda b, h: (b, 0, h)),
            pl.BlockSpec((1, seq, HEAD_DIM), lambda b, h: (b, 0, N_HEADS + h)),
            pl.BlockSpec((1, seq, HEAD_DIM), lambda b, h: (b, 0, 2 * N_HEADS + h)),
        ],
        out_specs=pl.BlockSpec((1, seq, HEAD_DIM), lambda b, h: (b, 0, h)),
        out_shape=jax.ShapeDtypeStruct((n_b, seq, N_HEADS * HEAD_DIM), BF16),
        scratch_shapes=[pltpu.VMEM((seq, 2 * LANES), BF16),
                        pltpu.VMEM((seq, LANES), BF16),
                        pltpu.VMEM((2 * SUBLANES, HEAD_DIM), F32)],
        compiler_params=_compiler_params(("parallel", "arbitrary")),
    )(slopes, qkv, qkv, qkv)


def _moba_layer(h, gain, w_qkv, w_o, n_b, seq, next_gain):
    d_model = h.shape[1]
    (qkv,) = _qkv_proj(h, gain, w_qkv, TM_RESIDENT, d_model // QKV_CHUNK_ROWS)
    slopes = 2.0 ** (-8.0 * jnp.arange(1, N_HEADS + 1, dtype=F32) / N_HEADS)
    attn = _moba_attention(qkv.reshape(n_b, seq, 3 * d_model), slopes)
    return _out_proj(attn.reshape(n_b * seq, d_model), w_o, h, TM_SQUARE, next_gain)


S5_BLOCK_GROUPS = LANES // S5_GROUP
S5_PAIR_ROWS = 2 * S5_GROUP
S5_PAIRS = S5_BLOCK_GROUPS // 2


def _s5_pair_tiles(w, n_lb):
    n_g, n_k, n_c, n_p = w.shape
    w = w.reshape(n_lb, S5_PAIRS, 2, n_k, n_c, n_p)
    even = jnp.pad(w[:, :, 0], ((0, 0),) * 4 + ((0, n_p),))
    odd = jnp.pad(w[:, :, 1], ((0, 0),) * 4 + ((n_p, 0),))
    return jnp.concatenate([even, odd], axis=3).astype(BF16)


def _s5_weights(a_re, a_im, log_dt, b_re, b_im, c_re, c_im, n_levels):
    n_g, n_p = a_re.shape
    n_lb = n_g // S5_BLOCK_GROUPS
    t_len = S5_CHUNK
    a = lax.complex(a_re.astype(F32), a_im.astype(F32))
    dt = jnp.exp(log_dt.astype(F32))[:, None]
    a_dt = a * dt
    a_bar = jnp.exp(a_dt)
    b_bar = ((a_bar - 1.0) / a)[..., None] * lax.complex(b_re.astype(F32), b_im.astype(F32))
    c_mat = lax.complex(c_re.astype(F32), c_im.astype(F32))
    steps = jnp.arange(t_len + 1, dtype=F32)
    a_pow = jnp.exp(a_dt[:, None, :] * steps[None, :, None])

    inp_c = a_pow[:, t_len - 1::-1][:, :t_len, None, :] * b_bar.transpose(0, 2, 1)[:, None]
    out_c = c_mat[:, None] * a_pow[:, :, None, :]
    w_in = (_s5_pair_tiles(inp_c.real, n_lb), _s5_pair_tiles(inp_c.imag, n_lb))
    w_out = (_s5_pair_tiles(out_c.real, n_lb), _s5_pair_tiles(-out_c.imag, n_lb))

    jumps = (t_len * 2 ** jnp.arange(n_levels)).astype(F32)
    d = jnp.exp(a_dt[:, None, :] * jumps[None, :, None])
    d = d.reshape(n_lb, S5_BLOCK_GROUPS, n_levels, n_p).transpose(0, 2, 1, 3)
    d = d.reshape(n_lb, n_levels, S5_BLOCK_GROUPS * n_p)
    decay = jnp.stack([d.real, d.imag], axis=2)
    return w_in, w_out, decay


def _s5_scan(u, w_in, w_out, decay, d_skip, n_chunks):
    t_len, n_rows, d_model = u.shape
    n_levels = decay.shape[1]
    half = decay.shape[3]
    wide = t_len * LANES

    def body(u_ref, wre_ref, wim_ref, vre_ref, vim_ref, dec_ref, d_ref,
             z_ref, a_scr, conv_scr, win_scr, vpow_scr):
        @pl.when(pl.program_id(0) == 0)
        def _():
            conv_scr[...] = jnp.zeros_like(conv_scr)
            win_scr[...] = jnp.zeros_like(win_scr)
            vpow_scr[...] = jnp.zeros_like(vpow_scr)

        def place(dst, k, re_ref, im_ref):
            for q in range(S5_PAIRS):
                rows = slice(k * LANES + q * S5_PAIR_ROWS, k * LANES + (q + 1) * S5_PAIR_ROWS)
                dst[rows, q * LANES:(q + 1) * LANES] = re_ref[0, q, k]
                dst[rows, half + q * LANES:half + (q + 1) * LANES] = im_ref[0, q, k]

        for t in range(t_len):
            a_scr[:, t * LANES:(t + 1) * LANES] = u_ref[t].astype(BF16)
            place(win_scr, t, wre_ref, wim_ref)
        for k in range(t_len + 1):
            place(vpow_scr, k, vre_ref, vim_ref)
        taps = _qk(win_scr[(t_len - 1) * LANES:, :], vpow_scr[:wide, :]).astype(BF16)
        for t in range(t_len):
            for t2 in range(t, t_len):
                conv_scr[t * LANES:(t + 1) * LANES, t2 * LANES:(t2 + 1) * LANES] = (
                    taps[:, (t2 - t) * LANES:(t2 - t + 1) * LANES])

        a = a_scr[...]
        x = jnp.dot(a, win_scr[...], preferred_element_type=F32)
        h_re, h_im = x[:, :half], x[:, half:]
        chunk = lax.broadcasted_iota(jnp.int32, (n_rows, half), 0) % n_chunks

        def shifted(v, n):
            return jnp.where(chunk >= n, pltpu.roll(v, n, 0), 0.0)

        for lvl in range(n_levels):
            p_re, p_im = shifted(h_re, 2 ** lvl), shifted(h_im, 2 ** lvl)
            d_re, d_im = dec_ref[0, lvl, 0:1, :], dec_ref[0, lvl, 1:2, :]
            h_re, h_im = (h_re + d_re * p_re - d_im * p_im,
                          h_im + d_re * p_im + d_im * p_re)
        h_in = jnp.concatenate([shifted(h_re, 1), shifted(h_im, 1)], axis=1).astype(BF16)
        y_state = _qk(h_in, vpow_scr[LANES:, :])
        n_split = 4
        per = t_len // n_split
        for part in range(n_split):
            k_hi = (part + 1) * per * LANES
            cols = slice(part * per * LANES, k_hi)
            y = y_state[:, cols] + jnp.dot(a_scr[:, :k_hi], conv_scr[:k_hi, cols],
                                           preferred_element_type=F32)
            for t in range(part * per, (part + 1) * per):
                lo = (t - part * per) * LANES
                z = y[:, lo:lo + LANES] + d_ref[...] * u_ref[t]
                z_ref[t] = jax.nn.gelu(z).astype(BF16)

    def pair_spec(n_k):
        return pl.BlockSpec((1, S5_PAIRS, n_k, S5_PAIR_ROWS, LANES),
                            lambda i: (i, 0, 0, 0, 0))

    return pl.pallas_call(
        body,
        grid=(d_model // LANES,),
        in_specs=[
            pl.BlockSpec((t_len, n_rows, LANES), lambda i: (0, 0, i)),
            pair_spec(t_len), pair_spec(t_len), pair_spec(t_len + 1), pair_spec(t_len + 1),
            pl.BlockSpec((1, n_levels, 2, half), lambda i: (i, 0, 0, 0)),
            pl.BlockSpec((1, LANES), lambda i: (0, i)),
        ],
        out_specs=pl.BlockSpec((t_len, n_rows, LANES), lambda i: (0, 0, i)),
        out_shape=jax.ShapeDtypeStruct((t_len, n_rows, d_model), BF16),
        scratch_shapes=[pltpu.VMEM((n_rows, wide), BF16),
                        pltpu.VMEM((wide, wide), BF16),
                        pltpu.VMEM((wide, 2 * half), BF16),
                        pltpu.VMEM((wide + LANES, 2 * half), BF16)],
        compiler_params=_compiler_params(("arbitrary",)),
    )(u, *w_in, *w_out, decay, d_skip)


def _s5_layer(h, gain, w_in, a_re, a_im, log_dt, b_re, b_im, c_re, c_im, d_skip,
              w_glu, n_b, seq):
    m_rows, d_model = h.shape
    t_len = S5_CHUNK
    n_chunks = seq // t_len
    n_levels = max(1, math.ceil(math.log2(n_chunks)))
    hp = h.reshape(n_b, n_chunks, t_len, d_model).transpose(2, 0, 1, 3).reshape(m_rows, d_model)
    u = _norm_proj_resident(hp, gain, w_in, TM_SQUARE)
    s5_in, s5_out, decay = _s5_weights(a_re, a_im, log_dt, b_re, b_im, c_re, c_im, n_levels)
    z = _s5_scan(u.reshape(t_len, n_b * n_chunks, d_model), s5_in, s5_out, decay,
                 d_skip.astype(F32).reshape(1, d_model), n_chunks)

    def glu_epilogue(accs, a_tiles, _):
        return [a_tiles[1] + accs[0][:, :d_model] * _sigmoid(accs[0][:, d_model:])]

    (hp,) = _resident_matmul([(z.reshape(m_rows, d_model), ()), (hp, ())], [],
                             lambda a_tiles, _: [a_tiles[0]], [(0, *w_glu)], glu_epilogue,
                             [(d_model, F32)], TM_RESIDENT, d_model // GLU_CHUNK_ROWS)
    return hp.reshape(t_len, n_b, n_chunks, d_model).transpose(1, 2, 0, 3).reshape(m_rows, d_model)


@jax.jit
def _trunk(x, p, norm_g, final_g, w_ffn_in, w_ffn_out, w_ple_gate, w_ple_proj,
           fox_w_in, fox_b_f, fox_w_o, moba_w_qkv, moba_w_o,
           s5_w_in, s5_a_re, s5_a_im, s5_log_dt, s5_b_re, s5_b_im,
           s5_c_re, s5_c_im, s5_d, s5_w_glu):
    n_b, seq, d_model = x.shape
    depth = norm_g.shape[0]
    m_rows = n_b * seq
    tm = min(1024, m_rows)
    tn = 512
    h = x.reshape(m_rows, d_model)
    p_rows = p.reshape(depth, m_rows, p.shape[-1])
    xn = None
    for i in range(depth):
        g = norm_g[i]
        last = i == depth - 1
        h = _ffn(h, g[0], (w_ffn_in, (i, 0)), (w_ffn_out, (i, 0)), tm, tn, xn)
        kind, j = i % 3, i // 3
        if kind == 0:
            h, xn = _fox_layer(h, g[1], (fox_w_in, (j,)), fox_b_f[j], (fox_w_o, (j,)),
                               n_b, seq, g[2])
        elif kind == 1:
            h, xn = _moba_layer(h, g[1], (moba_w_qkv, (j,)), (moba_w_o, (j,)), n_b, seq, g[2])
        else:
            h, xn = _s5_layer(h, g[1], (s5_w_in, (j,)), s5_a_re[j], s5_a_im[j],
                              s5_log_dt[j], s5_b_re[j], s5_b_im[j], s5_c_re[j], s5_c_im[j],
                              s5_d[j], (s5_w_glu, (j,)), n_b, seq), None
        h = _ffn(h, g[2], (w_ffn_in, (i, 1)), (w_ffn_out, (i, 1)), tm, tn, xn)
        h, xn = _ple(h, g[3], (p_rows, (i,)), (w_ple_gate, (i,)), (w_ple_proj, (i,)),
                     TM_SQUARE, final_gain=final_g if last else None,
                     next_gain=None if last else norm_g[i + 1, 0])
    return h.reshape(n_b, seq, d_model)


def kernel(x, p, norm_g, final_g, w_ffn_in, w_ffn_out, w_ple_gate, w_ple_proj, fox_w_in, fox_b_f, fox_w_o, moba_w_qkv, moba_w_o, s5_w_in, s5_a_re, s5_a_im, s5_log_dt, s5_b_re, s5_b_im, s5_c_re, s5_c_im, s5_d, s5_w_glu):
    return _trunk(x, p, norm_g, final_g, w_ffn_in, w_ffn_out, w_ple_gate, w_ple_proj,
                  fox_w_in, fox_b_f, fox_w_o, moba_w_qkv, moba_w_o,
                  s5_w_in, s5_a_re, s5_a_im, s5_log_dt, s5_b_re, s5_b_im,
                  s5_c_re, s5_c_im, s5_d, s5_w_glu)
```

```python
import functools
import math

import jax
import jax.numpy as jnp
from jax import lax
from jax.experimental import pallas as pl
from jax.experimental.pallas import tpu as pltpu

F32 = jnp.float32
BF16 = jnp.bfloat16

N_HEADS = 16
HEAD_DIM = 128
LANES = 128
SUBLANES = 8
ATTN_BLOCK = 256
LOGITS_AHEAD = 2
MOBA_BLOCK = 256
MOBA_TOP_K = 3
S5_GROUP = 16
S5_STATE = 64
S5_CHUNK = 16
RMS_EPS = 1e-6
MASK_VALUE = -1e30
LOG2E = math.log2(math.e)
Q_PRESCALE = HEAD_DIM ** -0.5 * LOG2E

V7X_VMEM_BYTES = 64 * 1024 * 1024
VMEM_LIMIT_BYTES = V7X_VMEM_BYTES - 8 * 1024 * 1024
PROLOGUE_ROWS = 256
EPILOGUE_ROWS = 256
TM_RESIDENT = 256
WEIGHT_CHUNK_ROWS = 512
GLU_CHUNK_ROWS = 256
QKV_CHUNK_ROWS = 128
FOX_W_STAGES = 11
TM_SQUARE = 512
TM_FFN_NORMED = 2048


def _compiler_params(semantics):
    return pltpu.CompilerParams(dimension_semantics=semantics,
                                vmem_limit_bytes=VMEM_LIMIT_BYTES)


def _sigmoid(x):
    return 1.0 / (1.0 + jnp.exp(-x))


def _rms_rows(x, g):
    ms = jnp.mean(x * x, axis=-1, keepdims=True)
    return x * lax.rsqrt(ms + RMS_EPS) * g


def _prefixed_spec(prefix, block, index_map):
    return pl.BlockSpec((None,) * len(prefix) + block,
                        lambda *ids: prefix + index_map(*ids))


def _fused_matmul(a_list, row_consts, prologue, lhs_widths, terms, epilogue, out_dtype,
                  n_out, tm, tn):
    m_rows = a_list[0][0].shape[-2]
    grid = (m_rows // tm, n_out // tn)
    n_a, n_c, n_t = len(a_list), len(row_consts), len(terms)

    def body(*refs):
        a_refs = refs[:n_a]
        c_refs = refs[n_a:n_a + n_c]
        w_refs = refs[n_a + n_c:n_a + n_c + n_t]
        o_ref = refs[n_a + n_c + n_t]
        lhs_refs = refs[n_a + n_c + n_t + 1:]
        if prologue is not None:
            @pl.when(pl.program_id(1) == 0)
            def _():
                consts = [c[...] for c in c_refs]

                def rows_step(r, carry):
                    rows = pl.ds(pl.multiple_of(r * PROLOGUE_ROWS, PROLOGUE_ROWS),
                                 PROLOGUE_ROWS)
                    outs = prologue([a[rows, :] for a in a_refs], consts)
                    for dst, val in zip(lhs_refs, outs):
                        dst[rows, :] = val
                    return carry

                lax.fori_loop(0, tm // PROLOGUE_ROWS, rows_step, 0)
        else:
            lhs_refs = a_refs
        weights = [w[...].astype(BF16) for w in w_refs]
        for r0 in range(0, tm, EPILOGUE_ROWS):
            rows = slice(r0, r0 + EPILOGUE_ROWS)
            accs = [jnp.dot(lhs_refs[li][rows, :], w, preferred_element_type=F32)
                    for (li, _, _, _), w in zip(terms, weights)]
            o_ref[rows, :] = epilogue(accs).astype(out_dtype)

    in_specs = [_prefixed_spec(pre, (tm, a.shape[-1]), lambda i, j: (i, 0))
                for a, pre in a_list]
    in_specs += [pl.BlockSpec((1, c.shape[1]), lambda i, j: (0, 0)) for c in row_consts]
    for _, w, pre, off in terms:
        in_specs.append(_prefixed_spec(
            pre, (w.shape[-2], tn), functools.partial(lambda i, j, o: (0, j + o), o=off)))
    scratch = []
    if prologue is not None:
        scratch = [pltpu.VMEM((tm, k), BF16) for k in lhs_widths]
    return pl.pallas_call(
        body,
        grid=grid,
        in_specs=in_specs,
        out_specs=pl.BlockSpec((tm, tn), lambda i, j: (i, j)),
        out_shape=jax.ShapeDtypeStruct((m_rows, n_out), out_dtype),
        scratch_shapes=scratch,
        compiler_params=_compiler_params(("parallel", "arbitrary")),
    )(*[a for a, _ in a_list], *row_consts, *[w for _, w, _, _ in terms])


def _norm_prologue(a_tiles, consts):
    return [_rms_rows(a_tiles[0], consts[0]).astype(BF16)]


def _resident_matmul(a_list, row_consts, prologue, terms, epilogue, outs, tm, n_stage,
                     transposed=False, overlap=False):
    m_rows = a_list[0][0].shape[-2]
    n_a, n_c, n_t, n_o = len(a_list), len(row_consts), len(terms), len(outs)
    n_axis = -2 if transposed else -1
    padded = [-(-w.shape[n_axis] // LANES) * LANES for _, w, _ in terms]

    def scratch_shape(w, n_pad):
        return (n_pad, w.shape[-1]) if transposed else (w.shape[-2], n_pad)

    def body(*refs):
        a_refs = refs[:n_a]
        c_refs = refs[n_a:n_a + n_c]
        w_refs = refs[n_a + n_c:n_a + n_c + n_t]
        o_refs = refs[n_a + n_c + n_t:n_a + n_c + n_t + n_o]
        wb_refs = refs[n_a + n_c + n_t + n_o:n_a + n_c + 2 * n_t + n_o]
        acc_ref = refs[-1] if overlap else None
        g = pl.program_id(0)

        @pl.when(g == 0)
        def _():
            for (_, w, _), wb_ref, n_pad in zip(terms, wb_refs, padded):
                if n_pad == w.shape[n_axis]:
                    continue
                if transposed:
                    wb_ref[n_pad - LANES:, :] = jnp.zeros((LANES, w.shape[-1]), BF16)
                else:
                    wb_ref[:, n_pad - LANES:] = jnp.zeros((w.shape[-2], LANES), BF16)

        @pl.when(g < n_stage)
        def _():
            for (_, w, _), w_ref, wb_ref in zip(terms, w_refs, wb_refs):
                chunk = w.shape[-2] // n_stage
                rows = pl.ds(pl.multiple_of(g * chunk, chunk), chunk)
                wb_ref[rows, :w.shape[-1]] = w_ref[...].astype(BF16)
                if overlap:
                    cols = pl.ds(pl.multiple_of(g * chunk, chunk), chunk)
                    part = jnp.dot(a_refs[terms[0][0]][:, cols], wb_ref[rows, :],
                                   preferred_element_type=F32)

                    @pl.when(g == 0)
                    def _():
                        acc_ref[...] = part

                    @pl.when(g > 0)
                    def _():
                        acc_ref[...] += part

        def matmul(lhs, wb):
            if transposed:
                return _qk(lhs, wb)
            return jnp.dot(lhs, wb, preferred_element_type=F32)

        if overlap:
            @pl.when(g == n_stage - 1)
            def _():
                a_tiles = [a[...] for a in a_refs]
                consts = [c[...] for c in c_refs]
                tiles = epilogue([acc_ref[...]], a_tiles, consts)
                for o_ref, tile, (_, dtype) in zip(o_refs, tiles, outs):
                    o_ref[...] = tile.astype(dtype)

        @pl.when(g >= n_stage)
        def _():
            a_tiles = [a[...] for a in a_refs]
            consts = [c[...] for c in c_refs]
            lhs = prologue(a_tiles, consts)
            accs = [matmul(lhs[li], wb[...]) for (li, _, _), wb in zip(terms, wb_refs)]
            for o_ref, tile, (_, dtype) in zip(o_refs, epilogue(accs, a_tiles, consts), outs):
                o_ref[...] = tile.astype(dtype)

    skipped = 1 if overlap else 0

    def row_tile(g):
        return (jnp.maximum(g - n_stage + skipped, 0), 0)

    in_specs = [_prefixed_spec(pre, (tm, a.shape[-1]), row_tile) for a, pre in a_list]
    in_specs += [pl.BlockSpec((1, c.shape[1]), lambda g: (0, 0)) for c in row_consts]
    for _, w, pre in terms:
        in_specs.append(_prefixed_spec(pre, (w.shape[-2] // n_stage, w.shape[-1]),
                                       lambda g: (jnp.minimum(g, n_stage - 1), 0)))
    results = pl.pallas_call(
        body,
        grid=(n_stage - skipped + m_rows // tm,),
        in_specs=in_specs,
        out_specs=[pl.BlockSpec((tm, n), row_tile) for n, _ in outs],
        out_shape=[jax.ShapeDtypeStruct((m_rows, n), dtype) for n, dtype in outs],
        scratch_shapes=[pltpu.VMEM(scratch_shape(w, n_pad), BF16)
                        for (_, w, _), n_pad in zip(terms, padded)]
        + ([pltpu.VMEM((tm, padded[0]), F32)] if overlap else []),
        compiler_params=_compiler_params(("arbitrary",)),
    )(*[a for a, _ in a_list], *row_consts, *[w for _, w, _ in terms])
    return results


def _ffn(h, gain, w_in, w_out, tm, tn_in, xn=None):
    d_model = h.shape[1]
    d_ff = w_out[0].shape[-2]
    terms = [(0, *w_in, 0), (0, *w_in, d_ff // tn_in)]

    def act_epilogue(accs):
        gate, up = accs
        return gate * _sigmoid(gate) * up

    if xn is None:
        act = _fused_matmul([(h, ())], [gain.reshape(1, d_model)], _norm_prologue,
                            [d_model], terms, act_epilogue, BF16, d_ff, tm, tn_in)
    else:
        act = _fused_matmul([(xn, ())], [], None, [], terms, act_epilogue, BF16, d_ff,
                            min(TM_FFN_NORMED, h.shape[0]), tn_in)

    (out,) = _resident_matmul([(act, ()), (h, ())], [], lambda a_tiles, _: [a_tiles[0]],
                              [(0, *w_out)],
                              lambda accs, a_tiles, _: [a_tiles[1] + 0.5 * accs[0]],
                              [(d_model, F32)], TM_RESIDENT, d_ff // WEIGHT_CHUNK_ROWS,
                              overlap=True)
    return out


def _ple(h, gain, p_i, w_gate, w_proj, tm, final_gain=None, next_gain=None):
    d_model = h.shape[1]
    consts = [gain.reshape(1, d_model)]
    outs = [(d_model, F32)]
    if final_gain is not None:
        consts.append(final_gain.reshape(1, d_model))
    if next_gain is not None:
        consts.append(next_gain.reshape(1, d_model))
        outs.append((d_model, BF16))

    def prologue(a_tiles, consts):
        return [_rms_rows(a_tiles[0], consts[0]).astype(BF16), a_tiles[1].astype(BF16)]

    def epilogue(accs, a_tiles, consts):
        out = a_tiles[0] + _sigmoid(accs[0]) * accs[1]
        if final_gain is not None:
            return [_rms_rows(out, consts[1])]
        if next_gain is not None:
            return [out, _rms_rows(out, consts[1])]
        return [out]

    res = _resident_matmul([(h, ()), p_i], consts, prologue, [(0, *w_gate), (1, *w_proj)],
                           epilogue, outs, tm, d_model // WEIGHT_CHUNK_ROWS)
    return res[0], (res[1] if next_gain is not None else None)


def _out_proj(a, w, h, tm, next_gain):
    d_model = h.shape[1]

    def epilogue(accs, a_tiles, consts):
        out = a_tiles[1] + accs[0]
        return [out, _rms_rows(out, consts[0])]

    return _resident_matmul([(a, ()), (h, ())], [next_gain.reshape(1, d_model)],
                            lambda a_tiles, _: [a_tiles[0]], [(0, *w)], epilogue,
                            [(d_model, F32), (d_model, BF16)], tm,
                            d_model // WEIGHT_CHUNK_ROWS, overlap=True)


def _norm_proj_resident(h, gain, w, tm):
    d_model = h.shape[1]
    (out,) = _resident_matmul([(h, ())], [gain.reshape(1, d_model)], _norm_prologue,
                              [(0, *w)], lambda accs, _a, _c: [accs[0]],
                              [(w[0].shape[-1], F32)], tm, d_model // WEIGHT_CHUNK_ROWS)
    return out


def _qkv_proj(h, gain, w, tm, n_stage, forget_bias=None, transposed=False):
    d_model = h.shape[1]
    consts = [gain.reshape(1, d_model)]
    outs = [(3 * d_model, BF16)]
    if forget_bias is not None:
        consts.append(forget_bias)
        outs.append((LANES, F32))

    def epilogue(accs, _a, consts):
        acc = accs[0]
        qkv = jnp.concatenate([acc[:, :d_model] * Q_PRESCALE, acc[:, d_model:3 * d_model]],
                              axis=1)
        if forget_bias is None:
            return [qkv]
        z = acc[:, 3 * d_model:] + consts[1]
        return [qkv, jnp.minimum(z, 0.0) - jnp.log(1.0 + jnp.exp(-jnp.abs(z)))]

    return _resident_matmul([(h, ())], consts, _norm_prologue, [(0, *w)], epilogue, outs,
                            tm, n_stage, transposed=transposed)


def _qk(q, k):
    return lax.dot_general(q, k, (((1,), (1,)), ((), ())), preferred_element_type=F32)


def _bf16_pieces(x):
    hi = x.astype(BF16).astype(F32)
    rest = x - hi
    mid = rest.astype(BF16).astype(F32)
    return hi, mid, rest - mid


def _lane_table(lane, entries, default=0.0):
    out = default
    for idx, val in reversed(list(entries.items())):
        out = jnp.where(lane == idx, val, out)
    return out


def _attend_all_tiles(q_ref, qaug_ref, kaug_ref, v_ref, o_ref, n_blk):
    blk = ATTN_BLOCK

    def logits(qi):
        rows = slice(qi * blk, (qi + 1) * blk)
        q_full = jnp.concatenate([q_ref[0, rows, :], qaug_ref[rows, :]], axis=1)
        return _qk(q_full, kaug_ref[0:(qi + 1) * blk, :])

    lane = lax.broadcasted_iota(jnp.int32, (v_ref.shape[1], LANES), 1)
    v_ones = jnp.concatenate([v_ref[0], jnp.where(lane == 0, 1.0, 0.0).astype(BF16)], axis=1)
    order = list(reversed(range(n_blk)))
    pending = [logits(qi) for qi in order[:LOGITS_AHEAD]]
    for pos, qi in enumerate(order):
        if pos + LOGITS_AHEAD < n_blk:
            pending.append(logits(order[pos + LOGITS_AHEAD]))
        s = pending.pop(0)
        o_ref[0, qi * blk:(qi + 1) * blk, :] = _softmax_pv(s, v_ones, qi + 1).astype(BF16)


def _softmax_pv(s, v_ones, n_blocks):
    blk = ATTN_BLOCK
    n_keys = n_blocks * blk
    row = lax.broadcasted_iota(jnp.int32, (blk, blk), 0)
    col = lax.broadcasted_iota(jnp.int32, (blk, blk), 1)
    diag = jnp.where(row >= col, s[:, n_keys - blk:], MASK_VALUE)
    parts = [s[:, i * LANES:(i + 1) * LANES] for i in range((n_keys - blk) // LANES)]
    parts += [diag[:, :LANES], diag[:, LANES:]]
    m = jnp.max(functools.reduce(jnp.maximum, parts), axis=1, keepdims=True)
    p = jnp.concatenate([jnp.exp2(part - m).astype(BF16) for part in parts], axis=1)
    acc = jnp.dot(p, v_ones[0:n_keys, :], preferred_element_type=F32)
    return acc[:, :LANES] / acc[:, LANES:LANES + 1]


def _log_forget_cumsum(logf):
    n_b, seq, lanes = logf.shape

    def body(x_ref, o_ref):
        x = x_ref[0]
        row = lax.broadcasted_iota(jnp.int32, (seq, lanes), 0)
        shift = 1
        while shift < seq:
            x = x + jnp.where(row >= shift, pltpu.roll(x, shift, 0), 0.0)
            shift *= 2
        o_ref[0] = x

    return pl.pallas_call(
        body,
        grid=(n_b,),
        in_specs=[pl.BlockSpec((1, seq, lanes), lambda b: (b, 0, 0))],
        out_specs=pl.BlockSpec((1, seq, lanes), lambda b: (b, 0, 0)),
        out_shape=jax.ShapeDtypeStruct((n_b, seq, lanes), F32),
        compiler_params=_compiler_params(("parallel",)),
    )(logf)


def _fox_attention(qkv, c):
    n_b, seq, _ = qkv.shape
    blk = ATTN_BLOCK
    n_blk = seq // blk

    def body(q_ref, k_ref, v_ref, c_ref, o_ref, kaug_ref, qaug_ref):
        h = pl.program_id(1)

        for r in range(n_blk):
            rows = slice(r * blk, (r + 1) * blk)
            lane = lax.broadcasted_iota(jnp.int32, (blk, LANES), 1)
            c_head = jnp.sum(jnp.where(lane == h, c_ref[0, rows, :], 0.0),
                             axis=1, keepdims=True) * LOG2E
            hi, mid, lo = _bf16_pieces(c_head)
            q_aug = _lane_table(lane, {0: hi, 1: mid, 2: lo, 3: 1.0, 4: 1.0, 5: 1.0})
            k_aug = _lane_table(lane, {0: 1.0, 1: 1.0, 2: 1.0, 3: -hi, 4: -mid, 5: -lo})
            qaug_ref[rows, :] = q_aug.astype(BF16)
            kaug_ref[rows, :LANES] = k_ref[0, rows, :]
            kaug_ref[rows, LANES:] = k_aug.astype(BF16)
        _attend_all_tiles(q_ref, qaug_ref, kaug_ref, v_ref, o_ref, n_blk)

    return pl.pallas_call(
        body,
        grid=(n_b, N_HEADS),
        in_specs=[
            pl.BlockSpec((1, seq, HEAD_DIM), lambda b, h: (b, 0, h)),
            pl.BlockSpec((1, seq, HEAD_DIM), lambda b, h: (b, 0, N_HEADS + h)),
            pl.BlockSpec((1, seq, HEAD_DIM), lambda b, h: (b, 0, 2 * N_HEADS + h)),
            pl.BlockSpec((1, seq, LANES), lambda b, h: (b, 0, 0)),
        ],
        out_specs=pl.BlockSpec((1, seq, HEAD_DIM), lambda b, h: (b, 0, h)),
        out_shape=jax.ShapeDtypeStruct((n_b, seq, N_HEADS * HEAD_DIM), BF16),
        scratch_shapes=[pltpu.VMEM((seq, 2 * LANES), BF16),
                        pltpu.VMEM((seq, LANES), BF16)],
        compiler_params=_compiler_params(("parallel", "arbitrary")),
    )(qkv, qkv, qkv, c)


def _fox_layer(h, gain, w_in, b_f, w_o, n_b, seq, next_gain):
    d_model = h.shape[1]
    b_row = jnp.pad(b_f, (0, LANES - N_HEADS)).reshape(1, LANES)
    w_t = (jnp.swapaxes(w_in[0], -1, -2), w_in[1])
    qkv, logf = _qkv_proj(h, gain, w_t, TM_RESIDENT, FOX_W_STAGES, forget_bias=b_row,
                          transposed=True)
    c = _log_forget_cumsum(logf.reshape(n_b, seq, LANES))
    attn = _fox_attention(qkv.reshape(n_b, seq, 3 * d_model), c)
    return _out_proj(attn.reshape(n_b * seq, d_model), w_o, h, TM_SQUARE, next_gain)


def _moba_attention(qkv, slopes):
    n_b, seq, _ = qkv.shape
    blk = MOBA_BLOCK
    assert blk == ATTN_BLOCK
    n_blk = seq // blk
    assert n_blk <= SUBLANES
    k_lanes = [SUBLANES, SUBLANES + 1, SUBLANES + 2]
    q_lanes = [SUBLANES + 3, SUBLANES + 4, SUBLANES + 5]

    def body(slopes_ref, q_ref, k_ref, v_ref, o_ref, kaug_ref, qaug_ref, kmean_ref):
        h = pl.program_id(1)
        slope2 = slopes_ref[h] * LOG2E

        kmean_ref[...] = jnp.zeros_like(kmean_ref)
        for n in range(n_blk):
            rows = slice(n * blk, (n + 1) * blk)
            kb = k_ref[0, rows, :]
            kmean_ref[n:n + 1, :] = jnp.mean(kb.astype(F32), axis=0, keepdims=True)
            lane = lax.broadcasted_iota(jnp.int32, (blk, LANES), 1)
            pos = (lax.broadcasted_iota(jnp.int32, (blk, 1), 0) + n * blk).astype(F32)
            hi, mid, lo = _bf16_pieces(slope2 * pos)
            entries = {n: 1.0, k_lanes[0]: hi, k_lanes[1]: mid, k_lanes[2]: lo,
                       q_lanes[0]: 1.0, q_lanes[1]: 1.0, q_lanes[2]: 1.0}
            kaug_ref[rows, :LANES] = kb
            kaug_ref[rows, LANES:] = _lane_table(lane, entries).astype(BF16)

        gate = _qk(kmean_ref[...].astype(BF16), q_ref[0])[:SUBLANES]
        key_blk = lax.broadcasted_iota(jnp.int32, (SUBLANES, seq), 0)
        q_pos = lax.broadcasted_iota(jnp.int32, (SUBLANES, seq), 1)
        q_blk = q_pos // blk
        beaten = jnp.zeros((SUBLANES, seq), F32)
        for n2 in range(n_blk):
            rival = gate[n2:n2 + 1, :]
            wins = jnp.where(key_blk > n2, jnp.where(rival >= gate, 1.0, 0.0),
                             jnp.where(rival > gate, 1.0, 0.0))
            beaten = beaten + jnp.where(n2 < q_blk, wins, 0.0)
        bias = jnp.where(key_blk < q_blk,
                         jnp.where(beaten < MOBA_TOP_K, 0.0, MASK_VALUE),
                         jnp.where(key_blk == q_blk, 0.0, MASK_VALUE))
        hi, mid, lo = _bf16_pieces(-slope2 * q_pos.astype(F32))
        pieces = jnp.where(key_blk < 3, 1.0,
                           jnp.where(key_blk == 3, hi,
                                     jnp.where(key_blk == 4, mid,
                                               jnp.where(key_blk == 5, lo, 0.0))))
        aug_t = jnp.concatenate(
            [bias, pieces, jnp.zeros((LANES - 2 * SUBLANES, seq), F32)], axis=0)
        for qi in range(n_blk):
            rows = slice(qi * blk, (qi + 1) * blk)
            qaug_ref[rows, :] = aug_t[:, rows].T.astype(BF16)
        _attend_all_tiles(q_ref, qaug_ref, kaug_ref, v_ref, o_ref, n_blk)

    return pl.pallas_call(
        body,
        grid=(n_b, N_HEADS),
        in_specs=[
            pl.BlockSpec(memory_space=pltpu.SMEM),
            pl.BlockSpec((1, seq, HEAD_DIM), lambda b, h: (b, 0, h)),
            pl.BlockSpec((1, seq, HEAD_DIM), lambda b, h: (b, 0, N_HEADS + h)),
            pl.BlockSpec((1, seq, HEAD_DIM), lambda b, h: (b, 0, 2 * N_HEADS + h)),
        ],
        out_specs=pl.BlockSpec((1, seq, HEAD_DIM), lambda b, h: (b, 0, h)),
        out_shape=jax.ShapeDtypeStruct((n_b, seq, N_HEADS * HEAD_DIM), BF16),
        scratch_shapes=[pltpu.VMEM((seq, 2 * LANES), BF16),
                        pltpu.VMEM((seq, LANES), BF16),
                        pltpu.VMEM((2 * SUBLANES, HEAD_DIM), F32)],
        compiler_params=_compiler_params(("parallel", "arbitrary")),
    )(slopes, qkv, qkv, qkv)


def _moba_layer(h, gain, w_qkv, w_o, n_b, seq, next_gain):
    d_model = h.shape[1]
    (qkv,) = _qkv_proj(h, gain, w_qkv, TM_RESIDENT, d_model // QKV_CHUNK_ROWS)
    slopes = 2.0 ** (-8.0 * jnp.arange(1, N_HEADS + 1, dtype=F32) / N_HEADS)
    attn = _moba_attention(qkv.reshape(n_b, seq, 3 * d_model), slopes)
    return _out_proj(attn.reshape(n_b * seq, d_model), w_o, h, TM_SQUARE, next_gain)


S5_BLOCK_GROUPS = LANES // S5_GROUP
S5_PAIR_ROWS = 2 * S5_GROUP
S5_PAIRS = S5_BLOCK_GROUPS // 2


def _s5_pair_tiles(w, n_lb):
    n_g, n_k, n_c, n_p = w.shape
    w = w.reshape(n_lb, S5_PAIRS, 2, n_k, n_c, n_p)
    even = jnp.pad(w[:, :, 0], ((0, 0),) * 4 + ((0, n_p),))
    odd = jnp.pad(w[:, :, 1], ((0, 0),) * 4 + ((n_p, 0),))
    return jnp.concatenate([even, odd], axis=3).astype(BF16)


def _s5_weights(a_re, a_im, log_dt, b_re, b_im, c_re, c_im, n_levels):
    n_g, n_p = a_re.shape
    n_lb = n_g // S5_BLOCK_GROUPS
    t_len = S5_CHUNK
    a = lax.complex(a_re.astype(F32), a_im.astype(F32))
    dt = jnp.exp(log_dt.astype(F32))[:, None]
    a_dt = a * dt
    a_bar = jnp.exp(a_dt)
    b_bar = ((a_bar - 1.0) / a)[..., None] * lax.complex(b_re.astype(F32), b_im.astype(F32))
    c_mat = lax.complex(c_re.astype(F32), c_im.astype(F32))
    steps = jnp.arange(t_len + 1, dtype=F32)
    a_pow = jnp.exp(a_dt[:, None, :] * steps[None, :, None])

    inp_c = a_pow[:, t_len - 1::-1][:, :t_len, None, :] * b_bar.transpose(0, 2, 1)[:, None]
    out_c = c_mat[:, None] * a_pow[:, :, None, :]
    w_in = (_s5_pair_tiles(inp_c.real, n_lb), _s5_pair_tiles(inp_c.imag, n_lb))
    w_out = (_s5_pair_tiles(out_c.real, n_lb), _s5_pair_tiles(-out_c.imag, n_lb))

    jumps = (t_len * 2 ** jnp.arange(n_levels)).astype(F32)
    d = jnp.exp(a_dt[:, None, :] * jumps[None, :, None])
    d = d.reshape(n_lb, S5_BLOCK_GROUPS, n_levels, n_p).transpose(0, 2, 1, 3)
    d = d.reshape(n_lb, n_levels, S5_BLOCK_GROUPS * n_p)
    decay = jnp.stack([d.real, d.imag], axis=2)
    return w_in, w_out, decay


def _s5_scan(u, w_in, w_out, decay, d_skip, n_chunks):
    t_len, n_rows, d_model = u.shape
    n_levels = decay.shape[1]
    half = decay.shape[3]
    wide = t_len * LANES

    def body(u_ref, wre_ref, wim_ref, vre_ref, vim_ref, dec_ref, d_ref,
             z_ref, a_scr, conv_scr, win_scr, vpow_scr):
        @pl.when(pl.program_id(0) == 0)
        def _():
            conv_scr[...] = jnp.zeros_like(conv_scr)
            win_scr[...] = jnp.zeros_like(win_scr)
            vpow_scr[...] = jnp.zeros_like(vpow_scr)

        def place(dst, k, re_ref, im_ref):
            for q in range(S5_PAIRS):
                rows = slice(k * LANES + q * S5_PAIR_ROWS, k * LANES + (q + 1) * S5_PAIR_ROWS)
                dst[rows, q * LANES:(q + 1) * LANES] = re_ref[0, q, k]
                dst[rows, half + q * LANES:half + (q + 1) * LANES] = im_ref[0, q, k]

        for t in range(t_len):
            a_scr[:, t * LANES:(t + 1) * LANES] = u_ref[t].astype(BF16)
            place(win_scr, t, wre_ref, wim_ref)
        for k in range(t_len + 1):
            place(vpow_scr, k, vre_ref, vim_ref)
        taps = _qk(win_scr[(t_len - 1) * LANES:, :], vpow_scr[:wide, :]).astype(BF16)
        for t in range(t_len):
            for t2 in range(t, t_len):
                conv_scr[t * LANES:(t + 1) * LANES, t2 * LANES:(t2 + 1) * LANES] = (
                    taps[:, (t2 - t) * LANES:(t2 - t + 1) * LANES])

        a = a_scr[...]
        x = jnp.dot(a, win_scr[...], preferred_element_type=F32)
        h_re, h_im = x[:, :half], x[:, half:]
        chunk = lax.broadcasted_iota(jnp.int32, (n_rows, half), 0) % n_chunks

        def shifted(v, n):
            return jnp.where(chunk >= n, pltpu.roll(v, n, 0), 0.0)

        for lvl in range(n_levels):
            p_re, p_im = shifted(h_re, 2 ** lvl), shifted(h_im, 2 ** lvl)
            d_re, d_im = dec_ref[0, lvl, 0:1, :], dec_ref[0, lvl, 1:2, :]
            h_re, h_im = (h_re + d_re * p_re - d_im * p_im,
                          h_im + d_re * p_im + d_im * p_re)
        h_in = jnp.concatenate([shifted(h_re, 1), shifted(h_im, 1)], axis=1).astype(BF16)
        y_state = _qk(h_in, vpow_scr[LANES:, :])
        n_split = 8
        per = t_len // n_split
        for part in range(n_split):
            k_hi = (part + 1) * per * LANES
            cols = slice(part * per * LANES, k_hi)
            y = y_state[:, cols] + jnp.dot(a_scr[:, :k_hi], conv_scr[:k_hi, cols],
                                           preferred_element_type=F32)
            for t in range(part * per, (part + 1) * per):
                lo = (t - part * per) * LANES
                z = y[:, lo:lo + LANES] + d_ref[...] * u_ref[t]
                z_ref[t] = jax.nn.gelu(z).astype(BF16)

    def pair_spec(n_k):
        return pl.BlockSpec((1, S5_PAIRS, n_k, S5_PAIR_ROWS, LANES),
                            lambda i: (i, 0, 0, 0, 0))

    return pl.pallas_call(
        body,
        grid=(d_model // LANES,),
        in_specs=[
            pl.BlockSpec((t_len, n_rows, LANES), lambda i: (0, 0, i)),
            pair_spec(t_len), pair_spec(t_len), pair_spec(t_len + 1), pair_spec(t_len + 1),
            pl.BlockSpec((1, n_levels, 2, half), lambda i: (i, 0, 0, 0)),
            pl.BlockSpec((1, LANES), lambda i: (0, i)),
        ],
        out_specs=pl.BlockSpec((t_len, n_rows, LANES), lambda i: (0, 0, i)),
        out_shape=jax.ShapeDtypeStruct((t_len, n_rows, d_model), BF16),
        scratch_shapes=[pltpu.VMEM((n_rows, wide), BF16),
                        pltpu.VMEM((wide, wide), BF16),
                        pltpu.VMEM((wide, 2 * half), BF16),
                        pltpu.VMEM((wide + LANES, 2 * half), BF16)],
        compiler_params=_compiler_params(("arbitrary",)),
    )(u, *w_in, *w_out, decay, d_skip)


def _s5_layer(h, gain, w_in, a_re, a_im, log_dt, b_re, b_im, c_re, c_im, d_skip,
              w_glu, n_b, seq):
    m_rows, d_model = h.shape
    t_len = S5_CHUNK
    n_chunks = seq // t_len
    n_levels = max(1, math.ceil(math.log2(n_chunks)))
    hp = h.reshape(n_b, n_chunks, t_len, d_model).transpose(2, 0, 1, 3).reshape(m_rows, d_model)
    u = _norm_proj_resident(hp, gain, w_in, TM_SQUARE)
    s5_in, s5_out, decay = _s5_weights(a_re, a_im, log_dt, b_re, b_im, c_re, c_im, n_levels)
    z = _s5_scan(u.reshape(t_len, n_b * n_chunks, d_model), s5_in, s5_out, decay,
                 d_skip.astype(F32).reshape(1, d_model), n_chunks)

    def glu_epilogue(accs, a_tiles, _):
        return [a_tiles[1] + accs[0][:, :d_model] * _sigmoid(accs[0][:, d_model:])]

    (hp,) = _resident_matmul([(z.reshape(m_rows, d_model), ()), (hp, ())], [],
                             lambda a_tiles, _: [a_tiles[0]], [(0, *w_glu)], glu_epilogue,
                             [(d_model, F32)], TM_RESIDENT, d_model // GLU_CHUNK_ROWS,
                             overlap=True)
    return hp.reshape(t_len, n_b, n_chunks, d_model).transpose(1, 2, 0, 3).reshape(m_rows, d_model)


@jax.jit
def _trunk(x, p, norm_g, final_g, w_ffn_in, w_ffn_out, w_ple_gate, w_ple_proj,
           fox_w_in, fox_b_f, fox_w_o, moba_w_qkv, moba_w_o,
           s5_w_in, s5_a_re, s5_a_im, s5_log_dt, s5_b_re, s5_b_im,
           s5_c_re, s5_c_im, s5_d, s5_w_glu):
    n_b, seq, d_model = x.shape
    depth = norm_g.shape[0]
    m_rows = n_b * seq
    tm = min(1024, m_rows)
    tn = 512
    h = x.reshape(m_rows, d_model)
    p_rows = p.reshape(depth, m_rows, p.shape[-1])
    xn = None
    for i in range(depth):
        g = norm_g[i]
        last = i == depth - 1
        h = _ffn(h, g[0], (w_ffn_in, (i, 0)), (w_ffn_out, (i, 0)), tm, tn, xn)
        kind, j = i % 3, i // 3
        if kind == 0:
            h, xn = _fox_layer(h, g[1], (fox_w_in, (j,)), fox_b_f[j], (fox_w_o, (j,)),
                               n_b, seq, g[2])
        elif kind == 1:
            h, xn = _moba_layer(h, g[1], (moba_w_qkv, (j,)), (moba_w_o, (j,)), n_b, seq, g[2])
        else:
            h, xn = _s5_layer(h, g[1], (s5_w_in, (j,)), s5_a_re[j], s5_a_im[j],
                              s5_log_dt[j], s5_b_re[j], s5_b_im[j], s5_c_re[j], s5_c_im[j],
                              s5_d[j], (s5_w_glu, (j,)), n_b, seq), None
        h = _ffn(h, g[2], (w_ffn_in, (i, 1)), (w_ffn_out, (i, 1)), tm, tn, xn)
        h, xn = _ple(h, g[3], (p_rows, (i,)), (w_ple_gate, (i,)), (w_ple_proj, (i,)),
                     TM_SQUARE, final_gain=final_g if last else None,
                     next_gain=None if last else norm_g[i + 1, 0])
    return h.reshape(n_b, seq, d_model)


def kernel(x, p, norm_g, final_g, w_ffn_in, w_ffn_out, w_ple_gate, w_ple_proj, fox_w_in, fox_b_f, fox_w_o, moba_w_qkv, moba_w_o, s5_w_in, s5_a_re, s5_a_im, s5_log_dt, s5_b_re, s5_b_im, s5_c_re, s5_c_im, s5_d, s5_w_glu):
    return _trunk(x, p, norm_g, final_g, w_ffn_in, w_ffn_out, w_ple_gate, w_ple_proj,
                  fox_w_in, fox_b_f, fox_w_o, moba_w_qkv, moba_w_o,
                  s5_w_in, s5_a_re, s5_a_im, s5_log_dt, s5_b_re, s5_b_im,
                  s5_c_re, s5_c_im, s5_d, s5_w_glu)
```
